```python
import math
import jax
import jax.numpy as jnp
from jax import lax
import numpy as np

D_MODEL = 4096
BATCH = 1
SEQ = 16384
DEPTH = 2

NORM_EPS = 1e-6
BLOCK = 128
NEG_INF = -1e30

POOL_WINDOWS = (2, 4, 8, 16)
POOL_GROUP_DIM = 256
POOL_DIM = len(POOL_WINDOWS) * POOL_GROUP_DIM

MLA_HEADS = 24
MLA_Q_RANK = 1024
MLA_KV_RANK = 512
MLA_NOPE_DIM = 128
MLA_ROPE_DIM = 64
MLA_V_DIM = 128
ROPE_THETA = 10000.0

SGU_GROUPS = 8
SGU_GROUP_DIM = 256
SGU_DIM = SGU_GROUPS * SGU_GROUP_DIM
SGU_CHUNK = 128

SWA_HEADS = 32
SWA_KV_HEADS = 8
SWA_HEAD_DIM = 64
SWA_WINDOW = 128

REL_BUCKETS = 32
REL_MAX_DIST = 128

MEM_LEN = 256
XATTN_HEADS = 4
XATTN_HEAD_DIM = 128
XATTN_DIM = XATTN_HEADS * XATTN_HEAD_DIM

N_GROUPS = 4
EXPERTS_PER_GROUP = 8
N_EXPERTS = N_GROUPS * EXPERTS_PER_GROUP
TOP_K = 2
EXPERT_FF = 512
MOE_TOKEN_BLOCK = 128

EVEN_IN = POOL_DIM + MLA_Q_RANK + MLA_KV_RANK + MLA_ROPE_DIM
EVEN_OUT = POOL_DIM + MLA_HEADS * MLA_V_DIM
ODD_IN = 2 * SGU_DIM + (SWA_HEADS + 2 * SWA_KV_HEADS) * SWA_HEAD_DIM
ODD_OUT = SGU_DIM + SWA_HEADS * SWA_HEAD_DIM
N_EVEN = (DEPTH + 1) // 2
N_ODD = DEPTH // 2

kernel_name = 'hybrid_pool_mla_sgu_swa_hmoe_trunk'


def rms_norm(x, g):
    xf = x.astype(jnp.float32)
    y = xf * lax.rsqrt(jnp.mean(xf * xf, axis=-1, keepdims=True) + NORM_EPS)
    return (y * g.astype(jnp.float32)).astype(x.dtype)


def layer_norm(x, g, b):
    xf = x.astype(jnp.float32)
    mu = jnp.mean(xf, axis=-1, keepdims=True)
    xc = xf - mu
    y = xc * lax.rsqrt(jnp.mean(xc * xc, axis=-1, keepdims=True) + NORM_EPS)
    return (y * g.astype(jnp.float32) + b.astype(jnp.float32)).astype(x.dtype)


def rope_tables(seq_len):
    inv_freq = ROPE_THETA ** (-jnp.arange(0, MLA_ROPE_DIM, 2, dtype=jnp.float32) / MLA_ROPE_DIM)
    ang = jnp.arange(seq_len, dtype=jnp.float32)[:, None] * inv_freq[None, :]
    return jnp.cos(ang), jnp.sin(ang)


def apply_rope(x, cos, sin):
    x1, x2 = jnp.split(x.astype(jnp.float32), 2, axis=-1)
    return jnp.concatenate([x1 * cos - x2 * sin, x2 * cos + x1 * sin], axis=-1).astype(x.dtype)


def multiscale_pool_mixer(a, pool_w, pool_scale):
    S = a.shape[1]
    af = a.astype(jnp.float32)
    csum = jnp.cumsum(af, axis=1)
    count = jnp.arange(1, S + 1, dtype=jnp.float32)[None, :, None]
    outs = []
    for g, win in enumerate(POOL_WINDOWS):
        sl = slice(g * POOL_GROUP_DIM, (g + 1) * POOL_GROUP_DIM)
        c = csum[..., sl]
        c_prev = jnp.pad(c, ((0, 0), (win, 0), (0, 0)))[:, :S]
        mean = (c - c_prev) / jnp.minimum(count, win)
        d = (mean - af[..., sl]).astype(a.dtype)
        outs.append(jnp.einsum('bsc,cd->bsd', d, pool_w[g]))
    return jnp.concatenate(outs, axis=-1) * pool_scale


def mla_mixer(c_q, c_kv, k_pe, q_norm, w_uq, kv_norm, w_ukv, cos, sin):
    B, S, _ = c_q.shape
    H = MLA_HEADS
    q = jnp.einsum('bsr,rf->bsf', rms_norm(c_q, q_norm), w_uq).reshape(B, S, H, MLA_NOPE_DIM + MLA_ROPE_DIM)
    q_nope = q[..., :MLA_NOPE_DIM]
    q_pe = apply_rope(q[..., MLA_NOPE_DIM:], cos[None, :, None], sin[None, :, None])
    kv = jnp.einsum('bsr,rf->bsf', rms_norm(c_kv, kv_norm), w_ukv).reshape(B, S, H, MLA_NOPE_DIM + MLA_V_DIM)
    k_nope = kv[..., :MLA_NOPE_DIM]
    v = kv[..., MLA_NOPE_DIM:]
    k_pe = apply_rope(k_pe, cos[None], sin[None])
    scale = (MLA_NOPE_DIM + MLA_ROPE_DIM) ** -0.5
    nb = S // BLOCK
    kpos = jnp.arange(S)

    def to_blocks(t):
        return jnp.moveaxis(t.reshape(B, nb, BLOCK, *t.shape[2:]), 1, 0)

    def block_attn(args):
        qn, qr, q0 = args
        s = (jnp.einsum('bqhd,bkhd->bhqk', qn, k_nope, preferred_element_type=jnp.float32)
             + jnp.einsum('bqhr,bkr->bhqk', qr, k_pe, preferred_element_type=jnp.float32)) * scale
        qpos = q0 + jnp.arange(BLOCK)
        s = jnp.where(kpos[None, :] <= qpos[:, None], s, NEG_INF)
        p = jax.nn.softmax(s, axis=-1).astype(v.dtype)
        return jnp.einsum('bhqk,bkhd->bqhd', p, v)

    starts = jnp.arange(nb, dtype=jnp.int32) * BLOCK
    o = lax.map(block_attn, (to_blocks(q_nope), to_blocks(q_pe), starts))
    return jnp.moveaxis(o, 0, 1).reshape(B, S, H * MLA_V_DIM)


def spatial_gating_mixer(z, ln_g, ln_b, w_s, b_s):
    B, S, _ = z.shape
    z = jax.nn.gelu(z)
    u, v = jnp.split(z, 2, axis=-1)
    v = layer_norm(v, ln_g, ln_b)
    nc = S // SGU_CHUNK
    vb = v.reshape(B, nc, SGU_CHUNK, SGU_GROUPS, SGU_GROUP_DIM)
    causal = jnp.tril(jnp.ones((SGU_CHUNK, SGU_CHUNK), dtype=bool))
    w = jnp.where(causal[None], w_s, 0)
    mixed = jnp.einsum('gts,bcsgd->bctgd', w, vb) + b_s.T[None, None, :, :, None]
    return u * mixed.reshape(B, S, SGU_DIM)


def t5_bucket(dist):
    max_exact = REL_BUCKETS // 2
    large = max_exact + (jnp.log(jnp.maximum(dist, 1).astype(jnp.float32) / max_exact)
                         / math.log(REL_MAX_DIST / max_exact) * (REL_BUCKETS - max_exact)).astype(jnp.int32)
    large = jnp.minimum(large, REL_BUCKETS - 1)
    return jnp.where(dist < max_exact, dist, large)


def sliding_window_attention(q, k, v, sinks, rel_bias):
    B, S, _ = q.shape
    nb = S // BLOCK
    G = SWA_HEADS // SWA_KV_HEADS
    qb = q.reshape(B, nb, BLOCK, SWA_KV_HEADS, G, SWA_HEAD_DIM)

    def band(t):
        tb = t.reshape(B, nb, BLOCK, SWA_KV_HEADS, SWA_HEAD_DIM)
        prev = jnp.pad(tb, ((0, 0), (1, 0), (0, 0), (0, 0), (0, 0)))[:, :nb]
        return jnp.concatenate([prev, tb], axis=2)

    kb, vb = band(k), band(v)
    s = jnp.einsum('bnqkgd,bnjkd->bnkgqj', qb, kb, preferred_element_type=jnp.float32) * (SWA_HEAD_DIM ** -0.5)
    qi = jnp.arange(BLOCK)[:, None] + BLOCK
    kj = jnp.arange(2 * BLOCK)[None, :]
    dist = qi - kj
    in_window = (dist >= 0) & (dist < SWA_WINDOW)
    bias = rel_bias[t5_bucket(jnp.maximum(dist, 0))]
    bias = jnp.transpose(bias, (2, 0, 1)).reshape(SWA_KV_HEADS, G, BLOCK, 2 * BLOCK).astype(jnp.float32)
    key_exists = (jnp.arange(nb)[:, None] > 0) | (kj >= BLOCK)
    mask = in_window[None] & key_exists[:, None, :]
    s = jnp.where(mask[None, :, None, None], s + bias, NEG_INF)
    sink = sinks.reshape(SWA_KV_HEADS, G).astype(jnp.float32)[None, None, :, :, None, None]
    sink = jnp.broadcast_to(sink, s.shape[:-1] + (1,))
    p = jax.nn.softmax(jnp.concatenate([s, sink], axis=-1), axis=-1)[..., :-1]
    o = jnp.einsum('bnkgqj,bnjkd->bnqkgd', p.astype(v.dtype), vb)
    return o.reshape(B, S, SWA_HEADS * SWA_HEAD_DIM)


def even_mixer(h, w_in, pool_w, pool_scale, q_norm, w_uq, kv_norm, w_ukv, w_out, cos, sin):
    z = jnp.einsum('bsd,df->bsf', h, w_in)
    a, c_q, c_kv, k_pe = jnp.split(z, [POOL_DIM, POOL_DIM + MLA_Q_RANK, POOL_DIM + MLA_Q_RANK + MLA_KV_RANK], axis=-1)
    ya = multiscale_pool_mixer(a, pool_w, pool_scale)
    yb = mla_mixer(c_q, c_kv, k_pe, q_norm, w_uq, kv_norm, w_ukv, cos, sin)
    return jnp.einsum('bsf,fd->bsd', jnp.concatenate([ya, yb], axis=-1), w_out)


def odd_mixer(h, w_in, ln_g, ln_b, w_s, b_s, sinks, rel_bias, w_out):
    z = jnp.einsum('bsd,df->bsf', h, w_in)
    q_off = 2 * SGU_DIM
    k_off = q_off + SWA_HEADS * SWA_HEAD_DIM
    v_off = k_off + SWA_KV_HEADS * SWA_HEAD_DIM
    zc, q, k, v = jnp.split(z, [q_off, k_off, v_off], axis=-1)
    yc = spatial_gating_mixer(zc, ln_g, ln_b, w_s, b_s)
    yd = sliding_window_attention(q, k, v, sinks, rel_bias)
    return jnp.einsum('bsf,fd->bsd', jnp.concatenate([yc, yd], axis=-1), w_out)


def memory_cross_attention(h, mem_n, wq, wk, wv, wo):
    B, S, _ = h.shape
    M = mem_n.shape[1]
    q = jnp.einsum('bsd,df->bsf', h, wq).reshape(B, S, XATTN_HEADS, XATTN_HEAD_DIM)
    k = jnp.einsum('bmd,df->bmf', mem_n, wk).reshape(B, M, XATTN_HEADS, XATTN_HEAD_DIM)
    v = jnp.einsum('bmd,df->bmf', mem_n, wv).reshape(B, M, XATTN_HEADS, XATTN_HEAD_DIM)
    s = jnp.einsum('bshd,bmhd->bhsm', q, k, preferred_element_type=jnp.float32) * (XATTN_HEAD_DIM ** -0.5)
    p = jax.nn.softmax(s, axis=-1).astype(v.dtype)
    o = jnp.einsum('bhsm,bmhd->bshd', p, v).reshape(B, S, XATTN_DIM)
    return jnp.einsum('bsf,fd->bsd', o, wo)


def hierarchical_moe(h, w_group, b_group, w_expert, b_expert, w_gate, w_up, w_down):
    B, S, D = h.shape
    t = h.reshape(B * S, D)
    g_logits = jnp.einsum('td,dg->tg', t, w_group, preferred_element_type=jnp.float32) + b_group.astype(jnp.float32)
    p_group = jax.nn.softmax(g_logits, axis=-1)
    p_top_group, group_idx = lax.top_k(p_group, 1)
    e_logits = (jnp.einsum('td,de->te', t, w_expert, preferred_element_type=jnp.float32)
                + b_expert.astype(jnp.float32)).reshape(-1, N_GROUPS, EXPERTS_PER_GROUP)
    idx = jnp.broadcast_to(group_idx[:, :, None], (t.shape[0], 1, EXPERTS_PER_GROUP))
    in_group = jnp.take_along_axis(e_logits, idx, axis=1)[:, 0]
    top_logits, local_idx = lax.top_k(in_group, TOP_K)
    gates = jax.nn.softmax(top_logits, axis=-1) * p_top_group
    expert_idx = group_idx * EXPERTS_PER_GROUP + local_idx
    combine = jnp.einsum('tk,tke->te', gates, jax.nn.one_hot(expert_idx, N_EXPERTS, dtype=jnp.float32))

    def expert_block(args):
        tb, cb = args
        a = jnp.einsum('td,edf->tef', tb, w_gate)
        b = jnp.einsum('td,edf->tef', tb, w_up)
        hid = jax.nn.silu(a) * b * cb[:, :, None].astype(tb.dtype)
        return jnp.einsum('tef,efd->td', hid, w_down)

    nt = t.shape[0] // MOE_TOKEN_BLOCK
    y = lax.map(expert_block, (t.reshape(nt, MOE_TOKEN_BLOCK, D), combine.reshape(nt, MOE_TOKEN_BLOCK, N_EXPERTS)))
    return y.reshape(B, S, D)


def _normal(key, shape, scale):
    return scale * jax.random.normal(key, shape, dtype=jnp.float32)


def setup_inputs(seed: int = 0) -> dict:
    key = jax.random.key(seed)
    keys = iter(list(jax.random.split(key, 40)))
    D = D_MODEL

    def w(shape, fan_in):
        return _normal(next(keys), shape, fan_in ** -0.5)

    def gain(shape):
        return 1.0 + _normal(next(keys), shape, 0.02)

    return {
        'x': _normal(next(keys), (BATCH, SEQ, D), 1.0),
        'mem': _normal(next(keys), (BATCH, MEM_LEN, D), 1.0),
        'norm_mix': gain((DEPTH, D)),
        'norm_xattn': gain((DEPTH, D)),
        'norm_ffn': gain((DEPTH, D)),
        'norm_mem': gain((DEPTH, D)),
        'final_norm': gain((D,)),
        'rel_bias': _normal(next(keys), (REL_BUCKETS, SWA_HEADS), 0.5),
        'e_w_in': w((N_EVEN, D, EVEN_IN), D),
        'pool_w': w((N_EVEN, len(POOL_WINDOWS), POOL_GROUP_DIM, POOL_GROUP_DIM), POOL_GROUP_DIM),
        'pool_scale': 1.0 + _normal(next(keys), (N_EVEN, POOL_DIM), 0.1),
        'mla_q_norm': gain((N_EVEN, MLA_Q_RANK)),
        'mla_w_uq': w((N_EVEN, MLA_Q_RANK, MLA_HEADS * (MLA_NOPE_DIM + MLA_ROPE_DIM)), MLA_Q_RANK),
        'mla_kv_norm': gain((N_EVEN, MLA_KV_RANK)),
        'mla_w_ukv': w((N_EVEN, MLA_KV_RANK, MLA_HEADS * (MLA_NOPE_DIM + MLA_V_DIM)), MLA_KV_RANK),
        'e_w_out': w((N_EVEN, EVEN_OUT, D), EVEN_OUT),
        'o_w_in': w((N_ODD, D, ODD_IN), D),
        'sgu_ln_g': gain((N_ODD, SGU_DIM)),
        'sgu_ln_b': _normal(next(keys), (N_ODD, SGU_DIM), 0.02),
        'sgu_w': w((N_ODD, SGU_GROUPS, SGU_CHUNK, SGU_CHUNK), SGU_CHUNK),
        'sgu_b': 1.0 + _normal(next(keys), (N_ODD, SGU_GROUPS, SGU_CHUNK), 0.1),
        'swa_sinks': _normal(next(keys), (N_ODD, SWA_HEADS), 0.5),
        'o_w_out': w((N_ODD, ODD_OUT, D), ODD_OUT),
        'xa_wq': w((DEPTH, D, XATTN_DIM), D),
        'xa_wk': w((DEPTH, D, XATTN_DIM), D),
        'xa_wv': w((DEPTH, D, XATTN_DIM), D),
        'xa_wo': w((DEPTH, XATTN_DIM, D), XATTN_DIM),
        'moe_w_group': w((DEPTH, D, N_GROUPS), D),
        'moe_b_group': _normal(next(keys), (DEPTH, N_GROUPS), 0.01),
        'moe_w_expert': w((DEPTH, D, N_EXPERTS), D),
        'moe_b_expert': _normal(next(keys), (DEPTH, N_EXPERTS), 0.01),
        'moe_w_gate': w((DEPTH, N_EXPERTS, D, EXPERT_FF), D),
        'moe_w_up': w((DEPTH, N_EXPERTS, D, EXPERT_FF), D),
        'moe_w_down': w((DEPTH, N_EXPERTS, EXPERT_FF, D), EXPERT_FF),
    }


def reference(x, mem, norm_mix, norm_xattn, norm_ffn, norm_mem, final_norm, rel_bias,
              e_w_in, pool_w, pool_scale, mla_q_norm, mla_w_uq, mla_kv_norm, mla_w_ukv, e_w_out,
              o_w_in, sgu_ln_g, sgu_ln_b, sgu_w, sgu_b, swa_sinks, o_w_out,
              xa_wq, xa_wk, xa_wv, xa_wo,
              moe_w_group, moe_b_group, moe_w_expert, moe_b_expert, moe_w_gate, moe_w_up, moe_w_down):
    cos, sin = rope_tables(x.shape[1])
    h = x
    for layer in range(DEPTH):
        i = layer // 2
        hn = rms_norm(h, norm_mix[layer])
        if layer % 2 == 0:
            mix = even_mixer(hn, e_w_in[i], pool_w[i], pool_scale[i], mla_q_norm[i], mla_w_uq[i],
                             mla_kv_norm[i], mla_w_ukv[i], e_w_out[i], cos, sin)
        else:
            mix = odd_mixer(hn, o_w_in[i], sgu_ln_g[i], sgu_ln_b[i], sgu_w[i], sgu_b[i],
                            swa_sinks[i], rel_bias, o_w_out[i])
        h = h + mix
        mem_n = rms_norm(mem, norm_mem[layer])
        h = h + memory_cross_attention(rms_norm(h, norm_xattn[layer]), mem_n,
                                       xa_wq[layer], xa_wk[layer], xa_wv[layer], xa_wo[layer])
        h = h + hierarchical_moe(rms_norm(h, norm_ffn[layer]), moe_w_group[layer], moe_b_group[layer],
                                 moe_w_expert[layer], moe_b_expert[layer], moe_w_gate[layer],
                                 moe_w_up[layer], moe_w_down[layer])
    return rms_norm(h, final_norm)
```

```python
import functools
import math

import jax
import jax.numpy as jnp
from jax import lax
from jax.experimental import pallas as pl
from jax.experimental.pallas import tpu as pltpu

F32 = jnp.float32
BF16 = jnp.bfloat16

NORM_EPS = 1e-6
NEG_INF = -1e30

POOL_WINDOWS = (2, 4, 8, 16)
POOL_GROUP_DIM = 256
POOL_DIM = 1024
POOL_HALO = 16

MLA_HEADS = 24
MLA_Q_RANK = 1024
MLA_KV_RANK = 512
MLA_NOPE_DIM = 128
MLA_ROPE_DIM = 64
MLA_V_DIM = 128
MLA_QK_DIM = MLA_NOPE_DIM + MLA_ROPE_DIM
ROPE_THETA = 10000.0

SGU_GROUPS = 8
SGU_GROUP_DIM = 256
SGU_DIM = 2048
SGU_CHUNK = 128

SWA_HEADS = 32
SWA_KV_HEADS = 8
SWA_HEAD_DIM = 64
SWA_BLOCK = 128
REL_BUCKETS = 32
REL_MAX_DIST = 128

XATTN_HEADS = 4
XATTN_HEAD_DIM = 128
XATTN_DIM = 512

N_GROUPS = 4
EXPERTS_PER_GROUP = 8
N_EXPERTS = 32
EXPERT_FF = 512
ROUTER_LANES = 128
EXPERT_TILE = 256

VMEM_LIMIT_BYTES = 56 * 1024 * 1024


def _params(*semantics):
    return pltpu.CompilerParams(dimension_semantics=semantics, vmem_limit_bytes=VMEM_LIMIT_BYTES)


def _rms(x, g):
    return x * lax.rsqrt(jnp.mean(x * x, axis=-1, keepdims=True) + NORM_EPS) * g


def _rmsnorm_kernel(x_ref, g_ref, o_ref):
    o_ref[...] = _rms(x_ref[...].astype(F32), g_ref[...]).astype(o_ref.dtype)


def _rmsnorm(x, g, out_dtype, tm):
    m, d = x.shape
    return pl.pallas_call(
        _rmsnorm_kernel,
        grid=(m // tm,),
        in_specs=[pl.BlockSpec((tm, d), lambda i: (i, 0)), pl.BlockSpec((1, d), lambda i: (0, 0))],
        out_specs=pl.BlockSpec((tm, d), lambda i: (i, 0)),
        out_shape=jax.ShapeDtypeStruct((m, d), out_dtype),
        compiler_params=_params("parallel"),
        name="rmsnorm",
    )(x, g.reshape(1, d).astype(F32))


def _matmul_kernel(*refs, n_in, has_res):
    o_ref = refs[-1]
    acc = jnp.dot(refs[0][...], refs[n_in][...], preferred_element_type=F32)
    for k in range(1, n_in):
        acc += jnp.dot(refs[k][...], refs[n_in + k][...], preferred_element_type=F32)
    if has_res:
        acc += refs[2 * n_in][...]
    o_ref[...] = acc.astype(o_ref.dtype)


def _matmul(xs, ws, out_dtype, tm, tn, residual=None, name="matmul"):
    m = xs[0].shape[0]
    n = ws[0].shape[1]
    in_specs = [pl.BlockSpec((tm, x.shape[1]), lambda i, j: (i, 0)) for x in xs]
    in_specs += [pl.BlockSpec((w.shape[0], tn), lambda i, j: (0, j)) for w in ws]
    args = list(xs) + list(ws)
    if residual is not None:
        in_specs.append(pl.BlockSpec((tm, tn), lambda i, j: (i, j)))
        args.append(residual)
    return pl.pallas_call(
        functools.partial(_matmul_kernel, n_in=len(xs), has_res=residual is not None),
        grid=(m // tm, n // tn),
        in_specs=in_specs,
        out_specs=pl.BlockSpec((tm, tn), lambda i, j: (i, j)),
        out_shape=jax.ShapeDtypeStruct((m, n), out_dtype),
        compiler_params=_params("parallel", "parallel"),
        name=name,
    )(*args)


def _pool_kernel(a_ref, halo_ref, w_ref, scale_ref, o_ref, *, ts):
    i = pl.program_id(0)
    g = pl.program_id(1)
    win = jnp.left_shift(2, g)
    a = a_ref[...]
    halo = jnp.where(i > 0, halo_ref[...], jnp.zeros_like(halo_ref))
    ext = jnp.concatenate([halo, a], axis=0)
    row = lax.broadcasted_iota(jnp.int32, (ts, POOL_HALO + ts), 0) + POOL_HALO
    col = lax.broadcasted_iota(jnp.int32, (ts, POOL_HALO + ts), 1)
    band = ((col <= row) & (col > row - win)).astype(BF16)
    wsum = jnp.dot(band, ext, preferred_element_type=F32)
    t = i * ts + lax.broadcasted_iota(jnp.int32, (ts, 1), 0)
    count = jnp.minimum(t + 1, win).astype(F32)
    d = (wsum / count - a.astype(F32)).astype(BF16)
    y = jnp.dot(d, w_ref[0], preferred_element_type=F32) * scale_ref[...]
    o_ref[...] = y.astype(o_ref.dtype)


def _pool_mixer(z, pool_w, pool_scale, ts):
    s = z.shape[0]
    c = POOL_GROUP_DIM
    hb = ts // POOL_HALO
    return pl.pallas_call(
        functools.partial(_pool_kernel, ts=ts),
        grid=(s // ts, len(POOL_WINDOWS)),
        in_specs=[
            pl.BlockSpec((ts, c), lambda i, g: (i, g)),
            pl.BlockSpec((POOL_HALO, c), lambda i, g: (jnp.maximum(i * hb - 1, 0), g)),
            pl.BlockSpec((1, c, c), lambda i, g: (g, 0, 0)),
            pl.BlockSpec((1, c), lambda i, g: (0, g)),
        ],
        out_specs=pl.BlockSpec((ts, c), lambda i, g: (i, g)),
        out_shape=jax.ShapeDtypeStruct((s, POOL_DIM), BF16),
        compiler_params=_params("parallel", "parallel"),
        name="pool_mixer",
    )(z, z, pool_w, pool_scale)


def _rope_half_block(blk, ck, sk):
    return blk * ck + pltpu.roll(blk, 64, 1) * sk


def _mla_q_kernel(cq_ref, g_ref, w_ref, ck_ref, sk_ref, o_ref, xn_ref, *, hb, scale):
    @pl.when(pl.program_id(1) == 0)
    def _():
        xn_ref[...] = _rms(cq_ref[...].astype(F32), g_ref[...]).astype(BF16)

    xn = xn_ref[...]
    for hh in range(hb):
        q = jnp.dot(xn, w_ref[hh], preferred_element_type=F32)
        o_ref[hh, :, 0:MLA_NOPE_DIM] = (q[:, :MLA_NOPE_DIM] * scale).astype(o_ref.dtype)
        pe = _rope_half_block(q[:, MLA_NOPE_DIM:], ck_ref[...], sk_ref[...]) * scale
        o_ref[hh, :, MLA_NOPE_DIM:MLA_QK_DIM] = pe[:, :MLA_ROPE_DIM].astype(o_ref.dtype)


def _mla_q(z, q_norm, w_q, ck, sk, tm, hb):
    s = z.shape[0]
    scale = MLA_QK_DIM ** -0.5
    return pl.pallas_call(
        functools.partial(_mla_q_kernel, hb=hb, scale=scale),
        grid=(s // tm, MLA_HEADS // hb),
        in_specs=[
            pl.BlockSpec((tm, MLA_Q_RANK), lambda i, j: (i, POOL_DIM // MLA_Q_RANK)),
            pl.BlockSpec((1, MLA_Q_RANK), lambda i, j: (0, 0)),
            pl.BlockSpec((hb, MLA_Q_RANK, 256), lambda i, j: (j, 0, 0)),
            pl.BlockSpec((tm, 128), lambda i, j: (i, 0)),
            pl.BlockSpec((tm, 128), lambda i, j: (i, 0)),
        ],
        out_specs=pl.BlockSpec((hb, tm, MLA_QK_DIM), lambda i, j: (j, i, 0)),
        out_shape=jax.ShapeDtypeStruct((MLA_HEADS, s, MLA_QK_DIM), BF16),
        scratch_shapes=[pltpu.VMEM((tm, MLA_Q_RANK), BF16)],
        compiler_params=_params("parallel", "arbitrary"),
        name="mla_q_proj",
    )(z, q_norm, w_q, ck, sk)


def _mla_kv_kernel(ckv_ref, kpe_ref, g_ref, w_ref, ck_ref, sk_ref, k_ref, v_ref, xn_ref, kr_ref,
                   *, hb):
    @pl.when(pl.program_id(1) == 0)
    def _():
        xn_ref[...] = _rms(ckv_ref[...].astype(F32), g_ref[...]).astype(BF16)
        kr = _rope_half_block(kpe_ref[...].astype(F32), ck_ref[...], sk_ref[...])
        kr_ref[...] = kr.astype(BF16)

    xn = xn_ref[...]
    for hh in range(hb):
        kv = jnp.dot(xn, w_ref[hh], preferred_element_type=F32)
        k_ref[hh, :, 0:MLA_NOPE_DIM] = kv[:, :MLA_NOPE_DIM].astype(k_ref.dtype)
        k_ref[hh, :, MLA_NOPE_DIM:MLA_QK_DIM] = kr_ref[:, 0:MLA_ROPE_DIM]
        v_ref[hh] = kv[:, MLA_NOPE_DIM:].astype(v_ref.dtype)


def _mla_kv(z, kv_norm, w_kv, ck, sk, tm, hb):
    s = z.shape[0]
    ckv_block = (POOL_DIM + MLA_Q_RANK) // MLA_KV_RANK
    kpe_block = (POOL_DIM + MLA_Q_RANK + MLA_KV_RANK) // 128
    return pl.pallas_call(
        functools.partial(_mla_kv_kernel, hb=hb),
        grid=(s // tm, MLA_HEADS // hb),
        in_specs=[
            pl.BlockSpec((tm, MLA_KV_RANK), lambda i, j: (i, ckv_block)),
            pl.BlockSpec((tm, 128), lambda i, j: (i, kpe_block)),
            pl.BlockSpec((1, MLA_KV_RANK), lambda i, j: (0, 0)),
            pl.BlockSpec((hb, MLA_KV_RANK, 256), lambda i, j: (j, 0, 0)),
            pl.BlockSpec((tm, 128), lambda i, j: (i, 0)),
            pl.BlockSpec((tm, 128), lambda i, j: (i, 0)),
        ],
        out_specs=[
            pl.BlockSpec((hb, tm, MLA_QK_DIM), lambda i, j: (j, i, 0)),
            pl.BlockSpec((hb, tm, MLA_V_DIM), lambda i, j: (j, i, 0)),
        ],
        out_shape=[
            jax.ShapeDtypeStruct((MLA_HEADS, s, MLA_QK_DIM), BF16),
            jax.ShapeDtypeStruct((MLA_HEADS, s, MLA_V_DIM), BF16),
        ],
        scratch_shapes=[pltpu.VMEM((tm, MLA_KV_RANK), BF16), pltpu.VMEM((tm, 128), BF16)],
        compiler_params=_params("parallel", "arbitrary"),
        name="mla_kv_proj",
    )(z, z, kv_norm, w_kv, ck, sk)


def _mla_attn_kernel(q_ref, k_ref, v_ref, o_ref, *, tq):
    i = pl.program_id(1)
    q = q_ref[0]

    def step(kv, carry, masked):
        m, l, acc = carry
        start = pl.multiple_of(kv * tq, tq)
        k = k_ref[0, pl.ds(start, tq), :]
        v = v_ref[0, pl.ds(start, tq), :]
        s = lax.dot_general(q, k, (((1,), (1,)), ((), ())), preferred_element_type=F32)
        if masked:
            row = lax.broadcasted_iota(jnp.int32, (tq, tq), 0)
            col = lax.broadcasted_iota(jnp.int32, (tq, tq), 1)
            s = jnp.where(col <= row, s, NEG_INF)
        m_new = jnp.maximum(m, jnp.max(s, axis=-1, keepdims=True))
        alpha = jnp.exp(m - m_new)
        p = jnp.exp(s - m_new)
        l = alpha * l + jnp.sum(p, axis=-1, keepdims=True)
        acc = alpha * acc + jnp.dot(p.astype(BF16), v, preferred_element_type=F32)
        return m_new, l, acc

    init = (jnp.full((tq, 1), NEG_INF, F32), jnp.zeros((tq, 1), F32),
            jnp.zeros((tq, MLA_V_DIM), F32))
    carry = lax.fori_loop(0, i, lambda kv, c: step(kv, c, False), init)
    _, l, acc = step(i, carry, True)
    o_ref[...] = (acc / l).astype(o_ref.dtype)


def _mla_attention(q, k, v, tq):
    s = q.shape[1]
    return pl.pallas_call(
        functools.partial(_mla_attn_kernel, tq=tq),
        grid=(MLA_HEADS, s // tq),
        in_specs=[
            pl.BlockSpec((1, tq, MLA_QK_DIM), lambda h, i: (h, i, 0)),
            pl.BlockSpec((1, s, MLA_QK_DIM), lambda h, i: (h, 0, 0)),
            pl.BlockSpec((1, s, MLA_V_DIM), lambda h, i: (h, 0, 0)),
        ],
        out_specs=pl.BlockSpec((tq, MLA_V_DIM), lambda h, i: (i, h)),
        out_shape=jax.ShapeDtypeStruct((s, MLA_HEADS * MLA_V_DIM), BF16),
        compiler_params=_params("parallel", "parallel"),
        name="mla_attention",
    )(q, k, v)


def _gelu_tanh(x):
    return 0.5 * x * (1.0 + jnp.tanh(math.sqrt(2.0 / math.pi) * (x + 0.044715 * (x * x * x))))


def _sgu_kernel(u_ref, v_ref, g_ref, b_ref, w_ref, bs_ref, o_ref, *, ts):
    row = lax.broadcasted_iota(jnp.int32, (SGU_CHUNK, SGU_CHUNK), 0)
    col = lax.broadcasted_iota(jnp.int32, (SGU_CHUNK, SGU_CHUNK), 1)
    causal = col <= row
    for c in range(ts // SGU_CHUNK):
        rows = slice(c * SGU_CHUNK, (c + 1) * SGU_CHUNK)
        v = _gelu_tanh(v_ref[rows, :].astype(F32))
        mu = jnp.mean(v, axis=-1, keepdims=True)
        vc = v - mu
        vn = vc * lax.rsqrt(jnp.mean(vc * vc, axis=-1, keepdims=True) + NORM_EPS)
        vn = (vn * g_ref[...] + b_ref[...]).astype(BF16)
        for g in range(SGU_GROUPS):
            cols = slice(g * SGU_GROUP_DIM, (g + 1) * SGU_GROUP_DIM)
            w = jnp.where(causal, w_ref[g], 0.0).astype(BF16)
            mixed = jnp.dot(w, vn[:, cols], preferred_element_type=F32) + bs_ref[:, g:g + 1]
            u = _gelu_tanh(u_ref[rows, cols].astype(F32))
            o_ref[rows, cols] = (u * mixed).astype(o_ref.dtype)


def _sgu_mixer(z, ln_g, ln_b, w_s, b_s_t, ts):
    s = z.shape[0]
    return pl.pallas_call(
        functools.partial(_sgu_kernel, ts=ts),
        grid=(s // ts,),
        in_specs=[
            pl.BlockSpec((ts, SGU_DIM), lambda i: (i, 0)),
            pl.BlockSpec((ts, SGU_DIM), lambda i: (i, 1)),
            pl.BlockSpec((1, SGU_DIM), lambda i: (0, 0)),
            pl.BlockSpec((1, SGU_DIM), lambda i: (0, 0)),
            pl.BlockSpec((SGU_GROUPS, SGU_CHUNK, SGU_CHUNK), lambda i: (0, 0, 0)),
            pl.BlockSpec((SGU_CHUNK, SGU_GROUPS), lambda i: (0, 0)),
        ],
        out_specs=pl.BlockSpec((ts, SGU_DIM), lambda i: (i, 0)),
        out_shape=jax.ShapeDtypeStruct((s, SGU_DIM), BF16),
        compiler_params=_params("parallel"),
        name="sgu_mixer",
    )(z, z, ln_g, ln_b, w_s, b_s_t)


def _swa_kernel(sink_ref, q_ref, kp_ref, kc_ref, vp_ref, vc_ref, bias_ref, o_ref):
    n = pl.program_id(0)
    blk = SWA_BLOCK
    hd = SWA_HEAD_DIM
    group = SWA_HEADS // SWA_KV_HEADS
    col = lax.broadcasted_iota(jnp.int32, (blk, 2 * blk), 1)
    no_prev = (n == 0) & (col < blk)
    scale = hd ** -0.5
    for kvh in range(SWA_KV_HEADS):
        cols = slice(kvh * hd, (kvh + 1) * hd)
        kb = jnp.concatenate([kp_ref[:, cols], kc_ref[:, cols]], axis=0)
        vb = jnp.concatenate([vp_ref[:, cols], vc_ref[:, cols]], axis=0)
        for g in range(group):
            h = kvh * group + g
            qh = q_ref[:, h * hd:(h + 1) * hd]
            s = lax.dot_general(qh, kb, (((1,), (1,)), ((), ())), preferred_element_type=F32)
            s = jnp.where(no_prev, NEG_INF, s * scale + bias_ref[h])
            sink = sink_ref[h]
            m = jnp.maximum(jnp.max(s, axis=-1, keepdims=True), sink)
            p = jnp.exp(s - m)
            denom = jnp.sum(p, axis=-1, keepdims=True) + jnp.exp(sink - m)
            o = jnp.dot(p.astype(BF16), vb, preferred_element_type=F32) / denom
            o_ref[:, h * hd:(h + 1) * hd] = o.astype(o_ref.dtype)


def _swa_mixer(z, sinks, bias):
    s = z.shape[0]
    blk = SWA_BLOCK
    qw = SWA_HEADS * SWA_HEAD_DIM
    kw = SWA_KV_HEADS * SWA_HEAD_DIM
    q_block = 2 * SGU_DIM // qw
    k_block = (2 * SGU_DIM + qw) // kw
    v_block = k_block + 1
    prev = lambda n: jnp.maximum(n - 1, 0)
    return pl.pallas_call(
        _swa_kernel,
        grid=(s // blk,),
        in_specs=[
            pl.BlockSpec(memory_space=pltpu.SMEM),
            pl.BlockSpec((blk, qw), lambda n: (n, q_block)),
            pl.BlockSpec((blk, kw), lambda n: (prev(n), k_block)),
            pl.BlockSpec((blk, kw), lambda n: (n, k_block)),
            pl.BlockSpec((blk, kw), lambda n: (prev(n), v_block)),
            pl.BlockSpec((blk, kw), lambda n: (n, v_block)),
            pl.BlockSpec((SWA_HEADS, blk, 2 * blk), lambda n: (0, 0, 0)),
        ],
        out_specs=pl.BlockSpec((blk, qw), lambda n: (n, 0)),
        out_shape=jax.ShapeDtypeStruct((s, qw), BF16),
        compiler_params=_params("parallel"),
        name="swa_mixer",
    )(sinks, z, z, z, z, z, bias)


def _t5_bucket(dist):
    max_exact = REL_BUCKETS // 2
    large = max_exact + (jnp.log(jnp.maximum(dist, 1).astype(F32) / max_exact)
                         / math.log(REL_MAX_DIST / max_exact) * (REL_BUCKETS - max_exact)).astype(jnp.int32)
    large = jnp.minimum(large, REL_BUCKETS - 1)
    return jnp.where(dist < max_exact, dist, large)


def _swa_bias(rel_bias):
    qi = jnp.arange(SWA_BLOCK)[:, None] + SWA_BLOCK
    kj = jnp.arange(2 * SWA_BLOCK)[None, :]
    dist = qi - kj
    in_window = (dist >= 0) & (dist < SWA_BLOCK)
    bias = rel_bias[_t5_bucket(jnp.maximum(dist, 0))].astype(F32)
    bias = jnp.where(in_window[:, :, None], bias, NEG_INF)
    return jnp.transpose(bias, (2, 0, 1))


def _xattn_kernel(h_ref, g_ref, wq_ref, kv_ref, wo_ref, o_ref):
    x = h_ref[...]
    xn = _rms(x, g_ref[...]).astype(BF16)
    q = jnp.dot(xn, wq_ref[...], preferred_element_type=F32) * (XATTN_HEAD_DIM ** -0.5)
    q = q.astype(BF16)
    outs = []
    for hd in range(XATTN_HEADS):
        cols = slice(hd * XATTN_HEAD_DIM, (hd + 1) * XATTN_HEAD_DIM)
        k = kv_ref[:, cols]
        v = kv_ref[:, XATTN_DIM + hd * XATTN_HEAD_DIM:XATTN_DIM + (hd + 1) * XATTN_HEAD_DIM]
        s = lax.dot_general(q[:, cols], k, (((1,), (1,)), ((), ())), preferred_element_type=F32)
        p = jnp.exp(s - jnp.max(s, axis=-1, keepdims=True))
        denom = jnp.sum(p, axis=-1, keepdims=True)
        o = jnp.dot(p.astype(BF16), v, preferred_element_type=F32) / denom
        outs.append(o.astype(BF16))
    o = jnp.concatenate(outs, axis=-1)
    o_ref[...] = x + jnp.dot(o, wo_ref[...], preferred_element_type=F32)


def _xattn(h, g, wq, kv_mem, wo, tm):
    s, d = h.shape
    mlen = kv_mem.shape[0]
    return pl.pallas_call(
        _xattn_kernel,
        grid=(s // tm,),
        in_specs=[
            pl.BlockSpec((tm, d), lambda i: (i, 0)),
            pl.BlockSpec((1, d), lambda i: (0, 0)),
            pl.BlockSpec((d, XATTN_DIM), lambda i: (0, 0)),
            pl.BlockSpec((mlen, 2 * XATTN_DIM), lambda i: (0, 0)),
            pl.BlockSpec((XATTN_DIM, d), lambda i: (0, 0)),
        ],
        out_specs=pl.BlockSpec((tm, d), lambda i: (i, 0)),
        out_shape=jax.ShapeDtypeStruct((s, d), F32),
        compiler_params=_params("parallel"),
        name="memory_xattn",
    )(h, g, wq, kv_mem, wo)


def _router_kernel(h_ref, g_ref, whi_ref, wlo_ref, b_ref, hn_ref, idx_ref, gate_ref, cnt_ref,
                   carry_ref, *, tm):
    @pl.when(pl.program_id(0) == 0)
    def _():
        carry_ref[...] = jnp.zeros_like(carry_ref)

    xn = _rms(h_ref[...], g_ref[...])
    hn_ref[...] = xn
    x_hi = xn.astype(BF16)
    x_lo = (xn - x_hi.astype(F32)).astype(BF16)
    logits = (jnp.dot(x_hi, whi_ref[...], preferred_element_type=F32)
              + jnp.dot(x_lo, whi_ref[...], preferred_element_type=F32)
              + jnp.dot(x_hi, wlo_ref[...], preferred_element_type=F32)) + b_ref[...]

    lane = lax.broadcasted_iota(jnp.int32, (tm, ROUTER_LANES), 1)
    big = jnp.int32(ROUTER_LANES)

    def first_argmax(vals):
        top = jnp.max(vals, axis=-1, keepdims=True)
        idx = jnp.min(jnp.where(vals == top, lane, big), axis=-1, keepdims=True)
        return top, idx

    is_group = lane < N_GROUPS
    g_logits = jnp.where(is_group, logits, NEG_INF)
    g_top, g_idx = first_argmax(g_logits)
    p_group = 1.0 / jnp.sum(jnp.where(is_group, jnp.exp(g_logits - g_top), 0.0), axis=-1,
                            keepdims=True)
    lo = N_GROUPS + g_idx * EXPERTS_PER_GROUP
    in_group = (lane >= lo) & (lane < lo + EXPERTS_PER_GROUP)
    e_logits = jnp.where(in_group, logits, NEG_INF)
    top1, i1 = first_argmax(e_logits)
    top2, i2 = first_argmax(jnp.where(lane == i1, NEG_INF, e_logits))
    e21 = jnp.exp(top2 - top1)
    gate1 = p_group / (1.0 + e21)
    gate2 = p_group * e21 / (1.0 + e21)

    chosen = ((lane == i1) | (lane == i2)).astype(BF16)
    r = lax.broadcasted_iota(jnp.int32, (tm, tm), 0)
    c = lax.broadcasted_iota(jnp.int32, (tm, tm), 1)
    before = (c < r).astype(BF16)
    rank = jnp.dot(before, chosen, preferred_element_type=F32) + carry_ref[...]
    carry_ref[...] += jnp.sum(chosen.astype(F32), axis=0, keepdims=True)
    cnt_ref[...] = carry_ref[...]
    r1 = jnp.sum(jnp.where(lane == i1, rank, 0.0), axis=-1, keepdims=True).astype(jnp.int32)
    r2 = jnp.sum(jnp.where(lane == i2, rank, 0.0), axis=-1, keepdims=True).astype(jnp.int32)

    idx_ref[...] = jnp.where(lane == 0, i1 - N_GROUPS,
                             jnp.where(lane == 1, i2 - N_GROUPS,
                                       jnp.where(lane == 2, r1, jnp.where(lane == 3, r2, 0))))
    gate_ref[...] = jnp.where(lane == 0, gate1, jnp.where(lane == 1, gate2, 0.0))


def _router(h, g, w_hi, w_lo, bias, tm):
    s, d = h.shape
    row = lambda i: (i, 0)
    fixed = lambda i: (0, 0)
    return pl.pallas_call(
        functools.partial(_router_kernel, tm=tm),
        grid=(s // tm,),
        in_specs=[
            pl.BlockSpec((tm, d), row),
            pl.BlockSpec((1, d), fixed),
            pl.BlockSpec((d, ROUTER_LANES), fixed),
            pl.BlockSpec((d, ROUTER_LANES), fixed),
            pl.BlockSpec((1, ROUTER_LANES), fixed),
        ],
        out_specs=[
            pl.BlockSpec((tm, d), row),
            pl.BlockSpec((tm, ROUTER_LANES), row),
            pl.BlockSpec((tm, ROUTER_LANES), row),
            pl.BlockSpec((1, ROUTER_LANES), fixed),
        ],
        out_shape=[
            jax.ShapeDtypeStruct((s, d), F32),
            jax.ShapeDtypeStruct((s, ROUTER_LANES), jnp.int32),
            jax.ShapeDtypeStruct((s, ROUTER_LANES), F32),
            jax.ShapeDtypeStruct((1, ROUTER_LANES), F32),
        ],
        scratch_shapes=[pltpu.VMEM((1, ROUTER_LANES), F32)],
        compiler_params=_params("arbitrary"),
        name="moe_router",
    )(h, g, w_hi, w_lo, bias)


def _row_copy(src_ref, src_row, dst_ref, dst_row, sem):
    return pltpu.make_async_copy(src_ref.at[pl.ds(src_row, 1)], dst_ref.at[pl.ds(dst_row, 1)], sem)


def _dispatch_kernel(pos_ref, hn_ref, xs_ref, sem, *, tb):
    s = hn_ref.shape[0]
    base = pl.program_id(0) * tb

    def issue(t, carry):
        _row_copy(hn_ref, base + t, xs_ref, pos_ref[base + t], sem).start()
        _row_copy(hn_ref, base + t, xs_ref, pos_ref[s + base + t], sem).start()
        return carry

    def drain(t, carry):
        _row_copy(hn_ref, base + t, xs_ref, pos_ref[base + t], sem).wait()
        _row_copy(hn_ref, base + t, xs_ref, pos_ref[s + base + t], sem).wait()
        return carry

    lax.fori_loop(0, tb, issue, 0)
    lax.fori_loop(0, tb, drain, 0)


def _dispatch(pos, hn, tb):
    s, d = hn.shape
    n_rows = pos.shape[0]
    return pl.pallas_call(
        functools.partial(_dispatch_kernel, tb=tb),
        grid_spec=pltpu.PrefetchScalarGridSpec(
            num_scalar_prefetch=1,
            grid=(s // tb,),
            in_specs=[pl.BlockSpec(memory_space=pl.ANY)],
            out_specs=pl.BlockSpec(memory_space=pl.ANY),
            scratch_shapes=[pltpu.SemaphoreType.DMA(())],
        ),
        out_shape=jax.ShapeDtypeStruct((n_rows, d), hn.dtype),
        compiler_params=_params("arbitrary"),
        name="moe_dispatch",
    )(pos, hn)


def _expert_kernel(tile_ref, exp_ref, lo_ref, hi_ref, first_ref, total_ref, x_ref, wg_ref, wu_ref,
                   wd_ref, o_ref):
    p = pl.program_id(0)

    @pl.when(p < total_ref[0])
    def _():
        x = x_ref[...].astype(BF16)
        a = jnp.dot(x, wg_ref[0], preferred_element_type=F32)
        b = jnp.dot(x, wu_ref[0], preferred_element_type=F32)
        row = lax.broadcasted_iota(jnp.int32, (x.shape[0], 1), 0)
        mine = (row >= lo_ref[p]) & (row < hi_ref[p])
        hid = jnp.where(mine, a * jax.nn.sigmoid(a) * b, 0.0).astype(BF16)
        y = jnp.dot(hid, wd_ref[0], preferred_element_type=F32)

        @pl.when(first_ref[p] == 1)
        def _():
            o_ref[...] = y

        @pl.when(first_ref[p] == 0)
        def _():
            o_ref[...] += y


def _experts(tables, xs, w_gate, w_up, w_down):
    n_rows, d = xs.shape
    tm = EXPERT_TILE
    rows = lambda p, tile, exp, *_: (tile[p], 0)
    wsel = lambda p, tile, exp, *_: (exp[p], 0, 0)
    return pl.pallas_call(
        _expert_kernel,
        grid_spec=pltpu.PrefetchScalarGridSpec(
            num_scalar_prefetch=len(tables),
            grid=(tables[0].shape[0],),
            in_specs=[
                pl.BlockSpec((tm, d), rows),
                pl.BlockSpec((1, d, EXPERT_FF), wsel),
                pl.BlockSpec((1, d, EXPERT_FF), wsel),
                pl.BlockSpec((1, EXPERT_FF, d), wsel),
            ],
            out_specs=pl.BlockSpec((tm, d), rows),
        ),
        out_shape=jax.ShapeDtypeStruct((n_rows, d), F32),
        compiler_params=_params("arbitrary"),
        name="moe_experts",
    )(*tables, xs, w_gate, w_up, w_down)


def _combine_kernel(pos_ref, h_ref, gate_ref, ys_ref, o_ref, y1_ref, y2_ref, sem, *, tb):
    s = h_ref.shape[0] * pl.num_programs(0)
    base = pl.program_id(0) * tb

    def issue(t, carry):
        _row_copy(ys_ref, pos_ref[base + t], y1_ref, t, sem).start()
        _row_copy(ys_ref, pos_ref[s + base + t], y2_ref, t, sem).start()
        return carry

    def drain(t, carry):
        _row_copy(ys_ref, pos_ref[base + t], y1_ref, t, sem).wait()
        _row_copy(ys_ref, pos_ref[s + base + t], y2_ref, t, sem).wait()
        return carry

    lax.fori_loop(0, tb, issue, 0)
    lax.fori_loop(0, tb, drain, 0)
    gates = gate_ref[...]
    o_ref[...] = h_ref[...] + gates[:, 0:1] * y1_ref[...] + gates[:, 1:2] * y2_ref[...]


def _combine(pos, h, gates, ys, tb):
    s, d = h.shape
    row = lambda i, pos: (i, 0)
    return pl.pallas_call(
        functools.partial(_combine_kernel, tb=tb),
        grid_spec=pltpu.PrefetchScalarGridSpec(
            num_scalar_prefetch=1,
            grid=(s // tb,),
            in_specs=[
                pl.BlockSpec((tb, d), row),
                pl.BlockSpec((tb, ROUTER_LANES), row),
                pl.BlockSpec(memory_space=pl.ANY),
            ],
            out_specs=pl.BlockSpec((tb, d), row),
            scratch_shapes=[pltpu.VMEM((tb, d), F32), pltpu.VMEM((tb, d), F32),
                            pltpu.SemaphoreType.DMA(())],
        ),
        out_shape=jax.ShapeDtypeStruct((s, d), F32),
        compiler_params=_params("arbitrary"),
        name="moe_combine",
    )(pos, h, gates, ys)


def _moe(h, g, w_group, b_group, w_expert, b_expert, w_gate, w_up, w_down):
    s, d = h.shape
    pad = ROUTER_LANES - N_GROUPS - N_EXPERTS
    w_r = jnp.pad(jnp.concatenate([w_group, w_expert], axis=1), ((0, 0), (0, pad)))
    w_hi = w_r.astype(BF16)
    w_lo = (w_r - w_hi.astype(F32)).astype(BF16)
    b_r = jnp.pad(jnp.concatenate([b_group, b_expert]), (0, pad)).reshape(1, ROUTER_LANES)
    hn, idx, gates, counts = _router(h, g, w_hi, w_lo, b_r.astype(F32), tm=min(512, s))

    tm = EXPERT_TILE
    n_tiles = 2 * s // tm
    n_pairs = n_tiles + N_EXPERTS - 1
    counts = counts[0, N_GROUPS:N_GROUPS + N_EXPERTS].astype(jnp.int32)
    seg_end = jnp.cumsum(counts)
    seg_start = seg_end - counts
    pos = jnp.concatenate([seg_start[idx[:, 0]] + idx[:, 2], seg_start[idx[:, 1]] + idx[:, 3]])
    tile_row = jnp.arange(n_tiles, dtype=jnp.int32) * tm
    first_e = jnp.searchsorted(seg_end, tile_row, side="right").astype(jnp.int32)
    last_e = jnp.searchsorted(seg_end, tile_row + tm - 1, side="right").astype(jnp.int32)
    per_tile = last_e - first_e + 1
    pair_end = jnp.cumsum(per_tile)
    pair_start = pair_end - per_tile
    total = pair_end[-1]
    p = jnp.minimum(jnp.arange(n_pairs, dtype=jnp.int32), total - 1)
    p_tile = jnp.searchsorted(pair_end, p, side="right").astype(jnp.int32)
    p_expert = first_e[p_tile] + p - pair_start[p_tile]
    p_lo = jnp.clip(seg_start[p_expert] - p_tile * tm, 0, tm)
    p_hi = jnp.clip(seg_end[p_expert] - p_tile * tm, 0, tm)
    p_first = (p_expert == first_e[p_tile]).astype(jnp.int32)

    xs = _dispatch(pos, hn, tb=min(512, s))
    ys = _experts((p_tile, p_expert, p_lo, p_hi, p_first, total.reshape(1)), xs,
                  w_gate.astype(BF16), w_up.astype(BF16), w_down.astype(BF16))
    return _combine(pos, h, gates, ys, tb=min(256, s))


def _rope_tables(s):
    inv_freq = ROPE_THETA ** (-jnp.arange(0, MLA_ROPE_DIM, 2, dtype=F32) / MLA_ROPE_DIM)
    ang = jnp.arange(s, dtype=F32)[:, None] * inv_freq[None, :]
    cos, sin = jnp.cos(ang), jnp.sin(ang)
    zeros = jnp.zeros((s, 128 - MLA_ROPE_DIM), F32)
    return (jnp.concatenate([cos, cos, zeros], axis=1), jnp.concatenate([sin, sin, zeros], axis=1))


def _rot_half_cols(w):
    half = MLA_ROPE_DIM // 2
    return jnp.concatenate([-w[..., half:], w[..., :half]], axis=-1)


def _even_mixer(h, hn, w_in, pool_w, pool_scale, q_norm, w_uq, kv_norm, w_ukv, w_out, ck, sk):
    s, d = h.shape
    w_kpe = w_in[:, POOL_DIM + MLA_Q_RANK + MLA_KV_RANK:]
    w_in_ext = jnp.concatenate([w_in, _rot_half_cols(w_kpe)], axis=1).astype(BF16)
    big = min(1024, s)
    z = _matmul([hn], [w_in_ext], BF16, tm=big, tn=896, name="even_in_proj")

    ya = _pool_mixer(z, pool_w.astype(BF16), pool_scale.reshape(1, POOL_DIM), ts=min(512, s))

    w_q = w_uq.reshape(MLA_Q_RANK, MLA_HEADS, MLA_QK_DIM)
    w_q = jnp.concatenate([w_q, _rot_half_cols(w_q[..., MLA_NOPE_DIM:])], axis=-1)
    w_q = jnp.transpose(w_q, (1, 0, 2)).astype(BF16)
    w_kv = jnp.transpose(w_ukv.reshape(MLA_KV_RANK, MLA_HEADS, MLA_NOPE_DIM + MLA_V_DIM),
                         (1, 0, 2)).astype(BF16)
    q = _mla_q(z, q_norm.reshape(1, -1), w_q, ck, sk, tm=big, hb=4)
    k, v = _mla_kv(z, kv_norm.reshape(1, -1), w_kv, ck, sk, tm=big, hb=4)
    yb = _mla_attention(q, k, v, tq=min(512, s))

    w_o = w_out.astype(BF16)
    return _matmul([ya, yb], [w_o[:POOL_DIM], w_o[POOL_DIM:]], F32, tm=min(512, s), tn=1024,
                   residual=h, name="even_out_proj")


def _odd_mixer(h, hn, w_in, ln_g, ln_b, w_s, b_s, sinks, rel_bias, w_out):
    s, d = h.shape
    big = min(1024, s)
    z = _matmul([hn], [w_in.astype(BF16)], BF16, tm=big, tn=1024, name="odd_in_proj")
    yc = _sgu_mixer(z, ln_g.reshape(1, -1), ln_b.reshape(1, -1), w_s, jnp.transpose(b_s),
                    ts=min(256, s))
    yd = _swa_mixer(z, sinks, _swa_bias(rel_bias))
    w_o = w_out.astype(BF16)
    return _matmul([yc, yd], [w_o[:SGU_DIM], w_o[SGU_DIM:]], F32, tm=min(512, s), tn=1024,
                   residual=h, name="odd_out_proj")


def kernel(x, mem, norm_mix, norm_xattn, norm_ffn, norm_mem, final_norm, rel_bias, e_w_in, pool_w, pool_scale, mla_q_norm, mla_w_uq, mla_kv_norm, mla_w_ukv, e_w_out, o_w_in, sgu_ln_g, sgu_ln_b, sgu_w, sgu_b, swa_sinks, o_w_out, xa_wq, xa_wk, xa_wv, xa_wo, moe_w_group, moe_b_group, moe_w_expert, moe_b_expert, moe_w_gate, moe_w_up, moe_w_down):
    batch, s, d = x.shape
    assert batch == 1
    depth = norm_mix.shape[0]
    ck, sk = _rope_tables(s)
    h = x.reshape(s, d)
    mem2 = mem.reshape(mem.shape[1], d)
    norm_tile = min(512, s)
    for layer in range(depth):
        i = layer // 2
        hn = _rmsnorm(h, norm_mix[layer], BF16, tm=norm_tile)
        if layer % 2 == 0:
            h = _even_mixer(h, hn, e_w_in[i], pool_w[i], pool_scale[i], mla_q_norm[i], mla_w_uq[i],
                            mla_kv_norm[i], mla_w_ukv[i], e_w_out[i], ck, sk)
        else:
            h = _odd_mixer(h, hn, o_w_in[i], sgu_ln_g[i], sgu_ln_b[i], sgu_w[i], sgu_b[i],
                           swa_sinks[i], rel_bias, o_w_out[i])
        mem_n = _rmsnorm(mem2, norm_mem[layer], BF16, tm=mem2.shape[0])
        w_kv_mem = jnp.concatenate([xa_wk[layer], xa_wv[layer]], axis=1).astype(BF16)
        kv_mem = _matmul([mem_n], [w_kv_mem], BF16, tm=mem2.shape[0], tn=2 * XATTN_DIM,
                         name="mem_kv_proj")
        h = _xattn(h, norm_xattn[layer].reshape(1, d), xa_wq[layer].astype(BF16), kv_mem,
                   xa_wo[layer].astype(BF16), tm=min(256, s))
        h = _moe(h, norm_ffn[layer].reshape(1, d), moe_w_group[layer], moe_b_group[layer],
                 moe_w_expert[layer], moe_b_expert[layer], moe_w_gate[layer], moe_w_up[layer],
                 moe_w_down[layer])
    out = _rmsnorm(h, final_norm, F32, tm=norm_tile)
    return out.reshape(batch, s, d)
```

```python
import functools
import math

import jax
import jax.numpy as jnp
from jax import lax
from jax.experimental import pallas as pl
from jax.experimental.pallas import tpu as pltpu

F32 = jnp.float32
BF16 = jnp.bfloat16

NORM_EPS = 1e-6
NEG_INF = -1e30

POOL_WINDOWS = (2, 4, 8, 16)
POOL_GROUP_DIM = 256
POOL_DIM = 1024
POOL_HALO = 16

MLA_HEADS = 24
MLA_Q_RANK = 1024
MLA_KV_RANK = 512
MLA_NOPE_DIM = 128
MLA_ROPE_DIM = 64
MLA_V_DIM = 128
MLA_QK_DIM = MLA_NOPE_DIM + MLA_ROPE_DIM
ROPE_THETA = 10000.0

SGU_GROUPS = 8
SGU_GROUP_DIM = 256
SGU_DIM = 2048
SGU_CHUNK = 128

SWA_HEADS = 32
SWA_KV_HEADS = 8
SWA_HEAD_DIM = 64
SWA_BLOCK = 128
REL_BUCKETS = 32
REL_MAX_DIST = 128

XATTN_HEADS = 4
XATTN_HEAD_DIM = 128
XATTN_DIM = 512

N_GROUPS = 4
EXPERTS_PER_GROUP = 8
N_EXPERTS = 32
EXPERT_FF = 512
ROUTER_LANES = 128
EXPERT_TILE = 256

VMEM_LIMIT_BYTES = 56 * 1024 * 1024


def _params(*semantics):
    return pltpu.CompilerParams(dimension_semantics=semantics, vmem_limit_bytes=VMEM_LIMIT_BYTES)


def _rms(x, g):
    return x * lax.rsqrt(jnp.mean(x * x, axis=-1, keepdims=True) + NORM_EPS) * g


def _rmsnorm_kernel(x_ref, g_ref, o_ref):
    o_ref[...] = _rms(x_ref[...].astype(F32), g_ref[...]).astype(o_ref.dtype)


def _rmsnorm(x, g, out_dtype, tm):
    m, d = x.shape
    return pl.pallas_call(
        _rmsnorm_kernel,
        grid=(m // tm,),
        in_specs=[pl.BlockSpec((tm, d), lambda i: (i, 0)), pl.BlockSpec((1, d), lambda i: (0, 0))],
        out_specs=pl.BlockSpec((tm, d), lambda i: (i, 0)),
        out_shape=jax.ShapeDtypeStruct((m, d), out_dtype),
        compiler_params=_params("parallel"),
        name="rmsnorm",
    )(x, g.reshape(1, d).astype(F32))


def _matmul_kernel(*refs, n_in, has_res):
    o_ref = refs[-1]
    acc = jnp.dot(refs[0][...], refs[n_in][...], preferred_element_type=F32)
    for k in range(1, n_in):
        acc += jnp.dot(refs[k][...], refs[n_in + k][...], preferred_element_type=F32)
    if has_res:
        acc += refs[2 * n_in][...]
    o_ref[...] = acc.astype(o_ref.dtype)


def _matmul(xs, ws, out_dtype, tm, tn, residual=None, name="matmul"):
    m = xs[0].shape[0]
    n = ws[0].shape[1]
    in_specs = [pl.BlockSpec((tm, x.shape[1]), lambda i, j: (i, 0)) for x in xs]
    in_specs += [pl.BlockSpec((w.shape[0], tn), lambda i, j: (0, j)) for w in ws]
    args = list(xs) + list(ws)
    if residual is not None:
        in_specs.append(pl.BlockSpec((tm, tn), lambda i, j: (i, j)))
        args.append(residual)
    return pl.pallas_call(
        functools.partial(_matmul_kernel, n_in=len(xs), has_res=residual is not None),
        grid=(m // tm, n // tn),
        in_specs=in_specs,
        out_specs=pl.BlockSpec((tm, tn), lambda i, j: (i, j)),
        out_shape=jax.ShapeDtypeStruct((m, n), out_dtype),
        compiler_params=_params("parallel", "parallel"),
        name=name,
    )(*args)


def _pool_kernel(a_ref, halo_ref, w_ref, scale_ref, o_ref, *, ts):
    i = pl.program_id(0)
    g = pl.program_id(1)
    win = jnp.left_shift(2, g)
    a = a_ref[...]
    halo = jnp.where(i > 0, halo_ref[...], jnp.zeros_like(halo_ref))
    ext = jnp.concatenate([halo, a], axis=0)
    row = lax.broadcasted_iota(jnp.int32, (ts, POOL_HALO + ts), 0) + POOL_HALO
    col = lax.broadcasted_iota(jnp.int32, (ts, POOL_HALO + ts), 1)
    band = ((col <= row) & (col > row - win)).astype(BF16)
    wsum = jnp.dot(band, ext, preferred_element_type=F32)
    t = i * ts + lax.broadcasted_iota(jnp.int32, (ts, 1), 0)
    count = jnp.minimum(t + 1, win).astype(F32)
    d = (wsum / count - a.astype(F32)).astype(BF16)
    y = jnp.dot(d, w_ref[0], preferred_element_type=F32) * scale_ref[...]
    o_ref[...] = y.astype(o_ref.dtype)


def _pool_mixer(z, pool_w, pool_scale, ts):
    s = z.shape[0]
    c = POOL_GROUP_DIM
    hb = ts // POOL_HALO
    return pl.pallas_call(
        functools.partial(_pool_kernel, ts=ts),
        grid=(s // ts, len(POOL_WINDOWS)),
        in_specs=[
            pl.BlockSpec((ts, c), lambda i, g: (i, g)),
            pl.BlockSpec((POOL_HALO, c), lambda i, g: (jnp.maximum(i * hb - 1, 0), g)),
            pl.BlockSpec((1, c, c), lambda i, g: (g, 0, 0)),
            pl.BlockSpec((1, c), lambda i, g: (0, g)),
        ],
        out_specs=pl.BlockSpec((ts, c), lambda i, g: (i, g)),
        out_shape=jax.ShapeDtypeStruct((s, POOL_DIM), BF16),
        compiler_params=_params("parallel", "parallel"),
        name="pool_mixer",
    )(z, z, pool_w, pool_scale)


def _rope_half_block(blk, ck, sk):
    return blk * ck + pltpu.roll(blk, 64, 1) * sk


def _mla_q_kernel(cq_ref, g_ref, w_ref, ck_ref, sk_ref, o_ref, xn_ref, *, hb, scale):
    @pl.when(pl.program_id(1) == 0)
    def _():
        xn_ref[...] = _rms(cq_ref[...].astype(F32), g_ref[...]).astype(BF16)

    xn = xn_ref[...]
    for hh in range(hb):
        q = jnp.dot(xn, w_ref[hh], preferred_element_type=F32)
        o_ref[hh, :, 0:MLA_NOPE_DIM] = (q[:, :MLA_NOPE_DIM] * scale).astype(o_ref.dtype)
        pe = _rope_half_block(q[:, MLA_NOPE_DIM:], ck_ref[...], sk_ref[...]) * scale
        o_ref[hh, :, MLA_NOPE_DIM:MLA_QK_DIM] = pe[:, :MLA_ROPE_DIM].astype(o_ref.dtype)


def _mla_q(z, q_norm, w_q, ck, sk, tm, hb):
    s = z.shape[0]
    scale = MLA_QK_DIM ** -0.5 * math.log2(math.e)
    return pl.pallas_call(
        functools.partial(_mla_q_kernel, hb=hb, scale=scale),
        grid=(s // tm, MLA_HEADS // hb),
        in_specs=[
            pl.BlockSpec((tm, MLA_Q_RANK), lambda i, j: (i, POOL_DIM // MLA_Q_RANK)),
            pl.BlockSpec((1, MLA_Q_RANK), lambda i, j: (0, 0)),
            pl.BlockSpec((hb, MLA_Q_RANK, 256), lambda i, j: (j, 0, 0)),
            pl.BlockSpec((tm, 128), lambda i, j: (i, 0)),
            pl.BlockSpec((tm, 128), lambda i, j: (i, 0)),
        ],
        out_specs=pl.BlockSpec((hb, tm, MLA_QK_DIM), lambda i, j: (j, i, 0)),
        out_shape=jax.ShapeDtypeStruct((MLA_HEADS, s, MLA_QK_DIM), BF16),
        scratch_shapes=[pltpu.VMEM((tm, MLA_Q_RANK), BF16)],
        compiler_params=_params("parallel", "arbitrary"),
        name="mla_q_proj",
    )(z, q_norm, w_q, ck, sk)


def _mla_kv_kernel(ckv_ref, kpe_ref, g_ref, w_ref, ck_ref, sk_ref, k_ref, v_ref, xn_ref, kr_ref,
                   *, hb):
    @pl.when(pl.program_id(1) == 0)
    def _():
        xn_ref[...] = _rms(ckv_ref[...].astype(F32), g_ref[...]).astype(BF16)
        kr = _rope_half_block(kpe_ref[...].astype(F32), ck_ref[...], sk_ref[...])
        kr_ref[...] = kr.astype(BF16)

    xn = xn_ref[...]
    for hh in range(hb):
        kv = jnp.dot(xn, w_ref[hh], preferred_element_type=F32)
        k_ref[hh, :, 0:MLA_NOPE_DIM] = kv[:, :MLA_NOPE_DIM].astype(k_ref.dtype)
        k_ref[hh, :, MLA_NOPE_DIM:MLA_QK_DIM] = kr_ref[:, 0:MLA_ROPE_DIM]
        v_ref[hh] = kv[:, MLA_NOPE_DIM:].astype(v_ref.dtype)


def _mla_kv(z, kv_norm, w_kv, ck, sk, tm, hb):
    s = z.shape[0]
    ckv_block = (POOL_DIM + MLA_Q_RANK) // MLA_KV_RANK
    kpe_block = (POOL_DIM + MLA_Q_RANK + MLA_KV_RANK) // 128
    return pl.pallas_call(
        functools.partial(_mla_kv_kernel, hb=hb),
        grid=(s // tm, MLA_HEADS // hb),
        in_specs=[
            pl.BlockSpec((tm, MLA_KV_RANK), lambda i, j: (i, ckv_block)),
            pl.BlockSpec((tm, 128), lambda i, j: (i, kpe_block)),
            pl.BlockSpec((1, MLA_KV_RANK), lambda i, j: (0, 0)),
            pl.BlockSpec((hb, MLA_KV_RANK, 256), lambda i, j: (j, 0, 0)),
            pl.BlockSpec((tm, 128), lambda i, j: (i, 0)),
            pl.BlockSpec((tm, 128), lambda i, j: (i, 0)),
        ],
        out_specs=[
            pl.BlockSpec((hb, tm, MLA_QK_DIM), lambda i, j: (j, i, 0)),
            pl.BlockSpec((hb, tm, MLA_V_DIM), lambda i, j: (j, i, 0)),
        ],
        out_shape=[
            jax.ShapeDtypeStruct((MLA_HEADS, s, MLA_QK_DIM), BF16),
            jax.ShapeDtypeStruct((MLA_HEADS, s, MLA_V_DIM), BF16),
        ],
        scratch_shapes=[pltpu.VMEM((tm, MLA_KV_RANK), BF16), pltpu.VMEM((tm, 128), BF16)],
        compiler_params=_params("parallel", "arbitrary"),
        name="mla_kv_proj",
    )(z, z, kv_norm, w_kv, ck, sk)


def _mla_attn_kernel(q_ref, k_ref, v_ref, o_ref, m_ref, l_ref, acc_ref, s_ref, *, tb, n_sub):
    i = pl.program_id(1)
    m_ref[...] = jnp.full_like(m_ref, NEG_INF)
    l_ref[...] = jnp.zeros_like(l_ref)
    acc_ref[...] = jnp.zeros_like(acc_ref)

    def scores(x, kv):
        start = pl.multiple_of(kv * tb, tb)
        q = q_ref[0, x * tb:(x + 1) * tb, :]
        k = k_ref[0, pl.ds(start, tb), :]
        return lax.dot_general(q, k, (((1,), (1,)), ((), ())), preferred_element_type=F32)

    def update(x, kv, s, masked):
        start = pl.multiple_of(kv * tb, tb)
        v = v_ref[0, pl.ds(start, tb), :]
        if masked:
            row = lax.broadcasted_iota(jnp.int32, (tb, tb), 0)
            col = lax.broadcasted_iota(jnp.int32, (tb, tb), 1)
            s = jnp.where(col <= row, s, NEG_INF)
        m_prev = m_ref[x]
        m_new = jnp.maximum(m_prev, jnp.max(s, axis=-1, keepdims=True))
        alpha = jnp.exp2(m_prev - m_new)
        p = jnp.exp2(s - jnp.concatenate([m_new] * (tb // 128), axis=1))
        l_ref[x] = alpha * l_ref[x] + jnp.sum(p, axis=-1, keepdims=True)
        acc_ref[x] = alpha * acc_ref[x] + jnp.dot(p.astype(BF16), v, preferred_element_type=F32)
        m_ref[x] = m_new

    for x in range(n_sub):
        s_ref[x] = scores(x, 0)

    def body(kv, carry):
        for x in range(n_sub):
            s = s_ref[x]
            s_ref[x] = scores(x, kv + 1)
            update(x, kv, s, False)
        return carry

    first_diag = n_sub * i
    lax.fori_loop(0, first_diag, body, 0)
    for d in range(n_sub):
        for x in range(d, n_sub):
            s = s_ref[x] if d == 0 else scores(x, first_diag + d)
            update(x, first_diag + d, s, x == d)
    for x in range(n_sub):
        o_ref[x * tb:(x + 1) * tb, :] = (acc_ref[x] / l_ref[x]).astype(o_ref.dtype)


def _mla_attention(q, k, v, tb, n_sub):
    s = q.shape[1]
    tq = tb * n_sub
    return pl.pallas_call(
        functools.partial(_mla_attn_kernel, tb=tb, n_sub=n_sub),
        grid=(MLA_HEADS, s // tq),
        in_specs=[
            pl.BlockSpec((1, tq, MLA_QK_DIM), lambda h, i: (h, i, 0)),
            pl.BlockSpec((1, s, MLA_QK_DIM), lambda h, i: (h, 0, 0)),
            pl.BlockSpec((1, s, MLA_V_DIM), lambda h, i: (h, 0, 0)),
        ],
        out_specs=pl.BlockSpec((tq, MLA_V_DIM), lambda h, i: (i, h)),
        out_shape=jax.ShapeDtypeStruct((s, MLA_HEADS * MLA_V_DIM), BF16),
        scratch_shapes=[pltpu.VMEM((n_sub, tb, 128), F32), pltpu.VMEM((n_sub, tb, 128), F32),
                        pltpu.VMEM((n_sub, tb, MLA_V_DIM), F32),
                        pltpu.VMEM((n_sub, tb, tb), F32)],
        compiler_params=_params("parallel", "arbitrary"),
        name="mla_attention",
    )(q, k, v)


def _gelu_tanh(x):
    return 0.5 * x * (1.0 + jnp.tanh(math.sqrt(2.0 / math.pi) * (x + 0.044715 * (x * x * x))))


def _sgu_kernel(u_ref, v_ref, g_ref, b_ref, w_ref, bs_ref, o_ref, *, ts):
    row = lax.broadcasted_iota(jnp.int32, (SGU_CHUNK, SGU_CHUNK), 0)
    col = lax.broadcasted_iota(jnp.int32, (SGU_CHUNK, SGU_CHUNK), 1)
    causal = col <= row
    for c in range(ts // SGU_CHUNK):
        rows = slice(c * SGU_CHUNK, (c + 1) * SGU_CHUNK)
        v = _gelu_tanh(v_ref[rows, :].astype(F32))
        mu = jnp.mean(v, axis=-1, keepdims=True)
        vc = v - mu
        vn = vc * lax.rsqrt(jnp.mean(vc * vc, axis=-1, keepdims=True) + NORM_EPS)
        vn = (vn * g_ref[...] + b_ref[...]).astype(BF16)
        for g in range(SGU_GROUPS):
            cols = slice(g * SGU_GROUP_DIM, (g + 1) * SGU_GROUP_DIM)
            w = jnp.where(causal, w_ref[g], 0.0).astype(BF16)
            mixed = jnp.dot(w, vn[:, cols], preferred_element_type=F32) + bs_ref[:, g:g + 1]
            u = _gelu_tanh(u_ref[rows, cols].astype(F32))
            o_ref[rows, cols] = (u * mixed).astype(o_ref.dtype)


def _sgu_mixer(z, ln_g, ln_b, w_s, b_s_t, ts):
    s = z.shape[0]
    return pl.pallas_call(
        functools.partial(_sgu_kernel, ts=ts),
        grid=(s // ts,),
        in_specs=[
            pl.BlockSpec((ts, SGU_DIM), lambda i: (i, 0)),
            pl.BlockSpec((ts, SGU_DIM), lambda i: (i, 1)),
            pl.BlockSpec((1, SGU_DIM), lambda i: (0, 0)),
            pl.BlockSpec((1, SGU_DIM), lambda i: (0, 0)),
            pl.BlockSpec((SGU_GROUPS, SGU_CHUNK, SGU_CHUNK), lambda i: (0, 0, 0)),
            pl.BlockSpec((SGU_CHUNK, SGU_GROUPS), lambda i: (0, 0)),
        ],
        out_specs=pl.BlockSpec((ts, SGU_DIM), lambda i: (i, 0)),
        out_shape=jax.ShapeDtypeStruct((s, SGU_DIM), BF16),
        compiler_params=_params("parallel"),
        name="sgu_mixer",
    )(z, z, ln_g, ln_b, w_s, b_s_t)


def _swa_kernel(sink_ref, q_ref, kp_ref, kc_ref, vp_ref, vc_ref, bias_ref, o_ref):
    n = pl.program_id(0)
    blk = SWA_BLOCK
    hd = SWA_HEAD_DIM
    group = SWA_HEADS // SWA_KV_HEADS
    col = lax.broadcasted_iota(jnp.int32, (blk, 2 * blk), 1)
    no_prev = (n == 0) & (col < blk)
    scale = hd ** -0.5
    for kvh in range(SWA_KV_HEADS):
        cols = slice(kvh * hd, (kvh + 1) * hd)
        kb = jnp.concatenate([kp_ref[:, cols], kc_ref[:, cols]], axis=0)
        vb = jnp.concatenate([vp_ref[:, cols], vc_ref[:, cols]], axis=0)
        for g in range(group):
            h = kvh * group + g
            qh = q_ref[:, h * hd:(h + 1) * hd]
            s = lax.dot_general(qh, kb, (((1,), (1,)), ((), ())), preferred_element_type=F32)
            s = jnp.where(no_prev, NEG_INF, s * scale + bias_ref[h])
            sink = sink_ref[h]
            m = jnp.maximum(jnp.max(s, axis=-1, keepdims=True), sink)
            p = jnp.exp(s - m)
            denom = jnp.sum(p, axis=-1, keepdims=True) + jnp.exp(sink - m)
            o = jnp.dot(p.astype(BF16), vb, preferred_element_type=F32) / denom
            o_ref[:, h * hd:(h + 1) * hd] = o.astype(o_ref.dtype)


def _swa_mixer(z, sinks, bias):
    s = z.shape[0]
    blk = SWA_BLOCK
    qw = SWA_HEADS * SWA_HEAD_DIM
    kw = SWA_KV_HEADS * SWA_HEAD_DIM
    q_block = 2 * SGU_DIM // qw
    k_block = (2 * SGU_DIM + qw) // kw
    v_block = k_block + 1
    prev = lambda n: jnp.maximum(n - 1, 0)
    return pl.pallas_call(
        _swa_kernel,
        grid=(s // blk,),
        in_specs=[
            pl.BlockSpec(memory_space=pltpu.SMEM),
            pl.BlockSpec((blk, qw), lambda n: (n, q_block)),
            pl.BlockSpec((blk, kw), lambda n: (prev(n), k_block)),
            pl.BlockSpec((blk, kw), lambda n: (n, k_block)),
            pl.BlockSpec((blk, kw), lambda n: (prev(n), v_block)),
            pl.BlockSpec((blk, kw), lambda n: (n, v_block)),
            pl.BlockSpec((SWA_HEADS, blk, 2 * blk), lambda n: (0, 0, 0)),
        ],
        out_specs=pl.BlockSpec((blk, qw), lambda n: (n, 0)),
        out_shape=jax.ShapeDtypeStruct((s, qw), BF16),
        compiler_params=_params("parallel"),
        name="swa_mixer",
    )(sinks, z, z, z, z, z, bias)


def _t5_bucket(dist):
    max_exact = REL_BUCKETS // 2
    large = max_exact + (jnp.log(jnp.maximum(dist, 1).astype(F32) / max_exact)
                         / math.log(REL_MAX_DIST / max_exact) * (REL_BUCKETS - max_exact)).astype(jnp.int32)
    large = jnp.minimum(large, REL_BUCKETS - 1)
    return jnp.where(dist < max_exact, dist, large)


def _swa_bias(rel_bias):
    qi = jnp.arange(SWA_BLOCK)[:, None] + SWA_BLOCK
    kj = jnp.arange(2 * SWA_BLOCK)[None, :]
    dist = qi - kj
    in_window = (dist >= 0) & (dist < SWA_BLOCK)
    bias = rel_bias[_t5_bucket(jnp.maximum(dist, 0))].astype(F32)
    bias = jnp.where(in_window[:, :, None], bias, NEG_INF)
    return jnp.transpose(bias, (2, 0, 1))


def _xattn_kernel(h_ref, g_ref, wq_ref, kv_ref, wo_ref, o_ref):
    x = h_ref[...]
    xn = _rms(x, g_ref[...]).astype(BF16)
    q = jnp.dot(xn, wq_ref[...], preferred_element_type=F32) * (XATTN_HEAD_DIM ** -0.5)
    q = q.astype(BF16)
    outs = []
    for hd in range(XATTN_HEADS):
        cols = slice(hd * XATTN_HEAD_DIM, (hd + 1) * XATTN_HEAD_DIM)
        k = kv_ref[:, cols]
        v = kv_ref[:, XATTN_DIM + hd * XATTN_HEAD_DIM:XATTN_DIM + (hd + 1) * XATTN_HEAD_DIM]
        s = lax.dot_general(q[:, cols], k, (((1,), (1,)), ((), ())), preferred_element_type=F32)
        p = jnp.exp(s - jnp.max(s, axis=-1, keepdims=True))
        denom = jnp.sum(p, axis=-1, keepdims=True)
        o = jnp.dot(p.astype(BF16), v, preferred_element_type=F32) / denom
        outs.append(o.astype(BF16))
    o = jnp.concatenate(outs, axis=-1)
    o_ref[...] = x + jnp.dot(o, wo_ref[...], preferred_element_type=F32)


def _xattn(h, g, wq, kv_mem, wo, tm):
    s, d = h.shape
    mlen = kv_mem.shape[0]
    return pl.pallas_call(
        _xattn_kernel,
        grid=(s // tm,),
        in_specs=[
            pl.BlockSpec((tm, d), lambda i: (i, 0)),
            pl.BlockSpec((1, d), lambda i: (0, 0)),
            pl.BlockSpec((d, XATTN_DIM), lambda i: (0, 0)),
            pl.BlockSpec((mlen, 2 * XATTN_DIM), lambda i: (0, 0)),
            pl.BlockSpec((XATTN_DIM, d), lambda i: (0, 0)),
        ],
        out_specs=pl.BlockSpec((tm, d), lambda i: (i, 0)),
        out_shape=jax.ShapeDtypeStruct((s, d), F32),
        compiler_params=_params("parallel"),
        name="memory_xattn",
    )(h, g, wq, kv_mem, wo)


def _router_kernel(h_ref, g_ref, whi_ref, wlo_ref, b_ref, idx_ref, gate_ref, cnt_ref, carry_ref,
                   *, tm):
    @pl.when(pl.program_id(0) == 0)
    def _():
        carry_ref[...] = jnp.zeros_like(carry_ref)

    xn = _rms(h_ref[...], g_ref[...])
    x_hi = xn.astype(BF16)
    x_lo = (xn - x_hi.astype(F32)).astype(BF16)
    logits = (jnp.dot(x_hi, whi_ref[...], preferred_element_type=F32)
              + jnp.dot(x_lo, whi_ref[...], preferred_element_type=F32)
              + jnp.dot(x_hi, wlo_ref[...], preferred_element_type=F32)) + b_ref[...]

    lane = lax.broadcasted_iota(jnp.int32, (tm, ROUTER_LANES), 1)
    big = jnp.int32(ROUTER_LANES)

    def first_argmax(vals):
        top = jnp.max(vals, axis=-1, keepdims=True)
        idx = jnp.min(jnp.where(vals == top, lane, big), axis=-1, keepdims=True)
        return top, idx

    is_group = lane < N_GROUPS
    g_logits = jnp.where(is_group, logits, NEG_INF)
    g_top, g_idx = first_argmax(g_logits)
    p_group = 1.0 / jnp.sum(jnp.where(is_group, jnp.exp(g_logits - g_top), 0.0), axis=-1,
                            keepdims=True)
    lo = N_GROUPS + g_idx * EXPERTS_PER_GROUP
    in_group = (lane >= lo) & (lane < lo + EXPERTS_PER_GROUP)
    e_logits = jnp.where(in_group, logits, NEG_INF)
    top1, i1 = first_argmax(e_logits)
    top2, i2 = first_argmax(jnp.where(lane == i1, NEG_INF, e_logits))
    e21 = jnp.exp(top2 - top1)
    gate1 = p_group / (1.0 + e21)
    gate2 = p_group * e21 / (1.0 + e21)

    chosen = ((lane == i1) | (lane == i2)).astype(BF16)
    r = lax.broadcasted_iota(jnp.int32, (tm, tm), 0)
    c = lax.broadcasted_iota(jnp.int32, (tm, tm), 1)
    before = (c < r).astype(BF16)
    rank = jnp.dot(before, chosen, preferred_element_type=F32) + carry_ref[...]
    carry_ref[...] += jnp.sum(chosen.astype(F32), axis=0, keepdims=True)
    cnt_ref[...] = carry_ref[...]
    r1 = jnp.sum(jnp.where(lane == i1, rank, 0.0), axis=-1, keepdims=True).astype(jnp.int32)
    r2 = jnp.sum(jnp.where(lane == i2, rank, 0.0), axis=-1, keepdims=True).astype(jnp.int32)

    idx_ref[...] = jnp.where(lane == 0, i1 - N_GROUPS,
                             jnp.where(lane == 1, i2 - N_GROUPS,
                                       jnp.where(lane == 2, r1, jnp.where(lane == 3, r2, 0))))
    gate_ref[...] = jnp.where(lane == 0, gate1, jnp.where(lane == 1, gate2, 0.0))


def _router(h, g, w_hi, w_lo, bias, tm):
    s, d = h.shape
    row = lambda i: (i, 0)
    fixed = lambda i: (0, 0)
    return pl.pallas_call(
        functools.partial(_router_kernel, tm=tm),
        grid=(s // tm,),
        in_specs=[
            pl.BlockSpec((tm, d), row),
            pl.BlockSpec((1, d), fixed),
            pl.BlockSpec((d, ROUTER_LANES), fixed),
            pl.BlockSpec((d, ROUTER_LANES), fixed),
            pl.BlockSpec((1, ROUTER_LANES), fixed),
        ],
        out_specs=[
            pl.BlockSpec((tm, ROUTER_LANES), row),
            pl.BlockSpec((tm, ROUTER_LANES), row),
            pl.BlockSpec((1, ROUTER_LANES), fixed),
        ],
        out_shape=[
            jax.ShapeDtypeStruct((s, ROUTER_LANES), jnp.int32),
            jax.ShapeDtypeStruct((s, ROUTER_LANES), F32),
            jax.ShapeDtypeStruct((1, ROUTER_LANES), F32),
        ],
        scratch_shapes=[pltpu.VMEM((1, ROUTER_LANES), F32)],
        compiler_params=_params("arbitrary"),
        name="moe_router",
    )(h, g, w_hi, w_lo, bias)


def _row_copy(src_ref, src_row, dst_ref, dst_row, sem):
    return pltpu.make_async_copy(src_ref.at[pl.ds(src_row, 1)], dst_ref.at[pl.ds(dst_row, 1)], sem)


def _pack_bf16_pair(hi, lo):
    hi_bits = lax.bitcast_convert_type(hi.astype(BF16).astype(F32), jnp.uint32)
    lo_bits = lax.bitcast_convert_type(lo.astype(BF16).astype(F32), jnp.uint32)
    return hi_bits | (lo_bits >> 16)


def _unpack_bf16_pair(packed):
    hi = lax.bitcast_convert_type(packed & jnp.uint32(0xFFFF0000), F32)
    lo = lax.bitcast_convert_type(packed << 16, F32)
    return hi, lo


def _wait_rows(src_ref, dst_ref, sem, n):
    pltpu.make_async_copy(src_ref.at[pl.ds(0, n)], dst_ref.at[pl.ds(0, n)], sem).wait()


def _dispatch_kernel(pos_ref, h_ref, g_ref, xs_ref, buf_ref, sem, *, tb, n_steps):
    i = pl.program_id(0)
    s = tb * n_steps
    slot = lax.rem(i, 2)
    half = h_ref.shape[1] // 2

    def wait_slot(sl):
        _wait_rows(buf_ref.at[sl], xs_ref, sem.at[sl], tb)
        _wait_rows(buf_ref.at[sl], xs_ref, sem.at[sl], tb)

    @pl.when(i >= 2)
    def _():
        wait_slot(slot)

    xn = _rms(h_ref[...], g_ref[...])
    buf_ref[slot] = _pack_bf16_pair(xn[:, :half], xn[:, half:])
    base = i * tb

    def issue(t, carry):
        _row_copy(buf_ref.at[slot], t, xs_ref, pos_ref[base + t], sem.at[slot]).start()
        _row_copy(buf_ref.at[slot], t, xs_ref, pos_ref[s + base + t], sem.at[slot]).start()
        return carry

    lax.fori_loop(0, tb, issue, 0)

    @pl.when(i == n_steps - 1)
    def _():
        if n_steps > 1:
            wait_slot(1 - slot)
        wait_slot(slot)


def _dispatch(pos, h, g, tb):
    s, d = h.shape
    n_rows = pos.shape[0]
    n_steps = s // tb
    return pl.pallas_call(
        functools.partial(_dispatch_kernel, tb=tb, n_steps=n_steps),
        grid_spec=pltpu.PrefetchScalarGridSpec(
            num_scalar_prefetch=1,
            grid=(n_steps,),
            in_specs=[pl.BlockSpec((tb, d), lambda i, pos: (i, 0)),
                      pl.BlockSpec((1, d), lambda i, pos: (0, 0))],
            out_specs=pl.BlockSpec(memory_space=pl.ANY),
            scratch_shapes=[pltpu.VMEM((2, tb, d // 2), jnp.uint32),
                            pltpu.SemaphoreType.DMA((2,))],
        ),
        out_shape=jax.ShapeDtypeStruct((n_rows, d // 2), jnp.uint32),
        compiler_params=_params("arbitrary"),
        name="moe_dispatch",
    )(pos, h, g)


def _expert_kernel(tile_ref, exp_ref, lo_ref, hi_ref, first_ref, total_ref, x_ref, wg_ref, wu_ref,
                   wd_ref, o_ref):
    p = pl.program_id(0)

    @pl.when(p < total_ref[0])
    def _():
        half = wg_ref.shape[1] // 2
        x_hi, x_lo = _unpack_bf16_pair(x_ref[...])
        x_hi, x_lo = x_hi.astype(BF16), x_lo.astype(BF16)
        a = (jnp.dot(x_hi, wg_ref[0, :half, :], preferred_element_type=F32)
             + jnp.dot(x_lo, wg_ref[0, half:, :], preferred_element_type=F32))
        b = (jnp.dot(x_hi, wu_ref[0, :half, :], preferred_element_type=F32)
             + jnp.dot(x_lo, wu_ref[0, half:, :], preferred_element_type=F32))
        row = lax.broadcasted_iota(jnp.int32, (x_hi.shape[0], 1), 0)
        mine = (row >= lo_ref[p]) & (row < hi_ref[p])
        hid = (a * jax.nn.sigmoid(a) * b).astype(BF16)
        y = jnp.dot(hid, wd_ref[0], preferred_element_type=F32)
        packed = _pack_bf16_pair(y[:, :half], y[:, half:])

        @pl.when(first_ref[p] == 1)
        def _():
            o_ref[...] = packed

        @pl.when(first_ref[p] == 0)
        def _():
            o_ref[...] = jnp.where(mine, packed, o_ref[...])


def _experts(tables, xs, w_gate, w_up, w_down):
    n_rows, half = xs.shape
    d = 2 * half
    tm = EXPERT_TILE
    rows = lambda p, tile, exp, *_: (tile[p], 0)
    wsel = lambda p, tile, exp, *_: (exp[p], 0, 0)
    return pl.pallas_call(
        _expert_kernel,
        grid_spec=pltpu.PrefetchScalarGridSpec(
            num_scalar_prefetch=len(tables),
            grid=(tables[0].shape[0],),
            in_specs=[
                pl.BlockSpec((tm, half), rows),
                pl.BlockSpec((1, d, EXPERT_FF), wsel),
                pl.BlockSpec((1, d, EXPERT_FF), wsel),
                pl.BlockSpec((1, EXPERT_FF, d), wsel),
            ],
            out_specs=pl.BlockSpec((tm, half), rows),
        ),
        out_shape=jax.ShapeDtypeStruct((n_rows, half), jnp.uint32),
        compiler_params=_params("arbitrary"),
        name="moe_experts",
    )(*tables, xs, w_gate, w_up, w_down)


def _combine_kernel(pos_ref, h_ref, gate_ref, ys_ref, o_ref, y_ref, sem, *, tb, n_steps):
    i = pl.program_id(0)
    s = tb * n_steps
    slot = lax.rem(i, 2)
    half = h_ref.shape[1] // 2

    def fetch(block, sl):
        base = block * tb

        def issue(t, carry):
            _row_copy(ys_ref, pos_ref[base + t], y_ref.at[sl, 0], t, sem.at[sl]).start()
            _row_copy(ys_ref, pos_ref[s + base + t], y_ref.at[sl, 1], t, sem.at[sl]).start()
            return carry

        lax.fori_loop(0, tb, issue, 0)

    @pl.when(i == 0)
    def _():
        fetch(0, 0)

    @pl.when(i + 1 < n_steps)
    def _():
        fetch(i + 1, 1 - slot)

    _wait_rows(ys_ref, y_ref.at[slot, 0], sem.at[slot], tb)
    _wait_rows(ys_ref, y_ref.at[slot, 1], sem.at[slot], tb)
    gates = gate_ref[...]
    g1, g2 = gates[:, 0:1], gates[:, 1:2]
    hi1, lo1 = _unpack_bf16_pair(y_ref[slot, 0])
    hi2, lo2 = _unpack_bf16_pair(y_ref[slot, 1])
    o_ref[:, :half] = h_ref[:, :half] + g1 * hi1 + g2 * hi2
    o_ref[:, half:] = h_ref[:, half:] + g1 * lo1 + g2 * lo2


def _combine(pos, h, gates, ys, tb):
    s, d = h.shape
    n_steps = s // tb
    row = lambda i, pos: (i, 0)
    return pl.pallas_call(
        functools.partial(_combine_kernel, tb=tb, n_steps=n_steps),
        grid_spec=pltpu.PrefetchScalarGridSpec(
            num_scalar_prefetch=1,
            grid=(n_steps,),
            in_specs=[
                pl.BlockSpec((tb, d), row),
                pl.BlockSpec((tb, ROUTER_LANES), row),
                pl.BlockSpec(memory_space=pl.ANY),
            ],
            out_specs=pl.BlockSpec((tb, d), row),
            scratch_shapes=[pltpu.VMEM((2, 2, tb, d // 2), jnp.uint32),
                            pltpu.SemaphoreType.DMA((2,))],
        ),
        out_shape=jax.ShapeDtypeStruct((s, d), F32),
        compiler_params=_params("arbitrary"),
        name="moe_combine",
    )(pos, h, gates, ys)


def _moe(h, g, w_group, b_group, w_expert, b_expert, w_gate, w_up, w_down):
    s, d = h.shape
    pad = ROUTER_LANES - N_GROUPS - N_EXPERTS
    w_r = jnp.pad(jnp.concatenate([w_group, w_expert], axis=1), ((0, 0), (0, pad)))
    w_hi = w_r.astype(BF16)
    w_lo = (w_r - w_hi.astype(F32)).astype(BF16)
    b_r = jnp.pad(jnp.concatenate([b_group, b_expert]), (0, pad)).reshape(1, ROUTER_LANES)
    idx, gates, counts = _router(h, g, w_hi, w_lo, b_r.astype(F32), tm=min(512, s))

    tm = EXPERT_TILE
    n_tiles = 2 * s // tm
    n_pairs = n_tiles + N_EXPERTS - 1
    counts = counts[0, N_GROUPS:N_GROUPS + N_EXPERTS].astype(jnp.int32)
    seg_end = jnp.cumsum(counts)
    seg_start = seg_end - counts
    pos = jnp.concatenate([seg_start[idx[:, 0]] + idx[:, 2], seg_start[idx[:, 1]] + idx[:, 3]])
    tile_row = jnp.arange(n_tiles, dtype=jnp.int32) * tm
    first_e = jnp.searchsorted(seg_end, tile_row, side="right").astype(jnp.int32)
    last_e = jnp.searchsorted(seg_end, tile_row + tm - 1, side="right").astype(jnp.int32)
    per_tile = last_e - first_e + 1
    pair_end = jnp.cumsum(per_tile)
    pair_start = pair_end - per_tile
    total = pair_end[-1]
    p = jnp.minimum(jnp.arange(n_pairs, dtype=jnp.int32), total - 1)
    p_tile = jnp.searchsorted(pair_end, p, side="right").astype(jnp.int32)
    p_expert = first_e[p_tile] + p - pair_start[p_tile]
    p_lo = jnp.clip(seg_start[p_expert] - p_tile * tm, 0, tm)
    p_hi = jnp.clip(seg_end[p_expert] - p_tile * tm, 0, tm)
    p_first = (p_expert == first_e[p_tile]).astype(jnp.int32)

    xs = _dispatch(pos, h, g, tb=min(256, s))
    ys = _experts((p_tile, p_expert, p_lo, p_hi, p_first, total.reshape(1)), xs,
                  w_gate.astype(BF16), w_up.astype(BF16), w_down.astype(BF16))
    return _combine(pos, h, gates, ys, tb=min(256, s))


def _rope_tables(s):
    inv_freq = ROPE_THETA ** (-jnp.arange(0, MLA_ROPE_DIM, 2, dtype=F32) / MLA_ROPE_DIM)
    ang = jnp.arange(s, dtype=F32)[:, None] * inv_freq[None, :]
    cos, sin = jnp.cos(ang), jnp.sin(ang)
    zeros = jnp.zeros((s, 128 - MLA_ROPE_DIM), F32)
    return (jnp.concatenate([cos, cos, zeros], axis=1), jnp.concatenate([sin, sin, zeros], axis=1))


def _rot_half_cols(w):
    half = MLA_ROPE_DIM // 2
    return jnp.concatenate([-w[..., half:], w[..., :half]], axis=-1)


def _even_mixer(h, hn, w_in, pool_w, pool_scale, q_norm, w_uq, kv_norm, w_ukv, w_out, ck, sk):
    s, d = h.shape
    w_kpe = w_in[:, POOL_DIM + MLA_Q_RANK + MLA_KV_RANK:]
    w_in_ext = jnp.concatenate([w_in, _rot_half_cols(w_kpe)], axis=1).astype(BF16)
    big = min(1024, s)
    z = _matmul([hn], [w_in_ext], BF16, tm=big, tn=896, name="even_in_proj")

    ya = _pool_mixer(z, pool_w.astype(BF16), pool_scale.reshape(1, POOL_DIM), ts=min(512, s))

    w_q = w_uq.reshape(MLA_Q_RANK, MLA_HEADS, MLA_QK_DIM)
    w_q = jnp.concatenate([w_q, _rot_half_cols(w_q[..., MLA_NOPE_DIM:])], axis=-1)
    w_q = jnp.transpose(w_q, (1, 0, 2)).astype(BF16)
    w_kv = jnp.transpose(w_ukv.reshape(MLA_KV_RANK, MLA_HEADS, MLA_NOPE_DIM + MLA_V_DIM),
                         (1, 0, 2)).astype(BF16)
    q = _mla_q(z, q_norm.reshape(1, -1), w_q, ck, sk, tm=big, hb=4)
    k, v = _mla_kv(z, kv_norm.reshape(1, -1), w_kv, ck, sk, tm=big, hb=4)
    yb = _mla_attention(q, k, v, tb=min(512, s), n_sub=4 if s >= 2048 else 1)

    w_o = w_out.astype(BF16)
    return _matmul([ya, yb], [w_o[:POOL_DIM], w_o[POOL_DIM:]], F32, tm=min(512, s), tn=1024,
                   residual=h, name="even_out_proj")


def _odd_mixer(h, hn, w_in, ln_g, ln_b, w_s, b_s, sinks, rel_bias, w_out):
    s, d = h.shape
    big = min(1024, s)
    z = _matmul([hn], [w_in.astype(BF16)], BF16, tm=big, tn=1024, name="odd_in_proj")
    yc = _sgu_mixer(z, ln_g.reshape(1, -1), ln_b.reshape(1, -1), w_s, jnp.transpose(b_s),
                    ts=min(256, s))
    yd = _swa_mixer(z, sinks, _swa_bias(rel_bias))
    w_o = w_out.astype(BF16)
    return _matmul([yc, yd], [w_o[:SGU_DIM], w_o[SGU_DIM:]], F32, tm=min(512, s), tn=1024,
                   residual=h, name="odd_out_proj")


def kernel(x, mem, norm_mix, norm_xattn, norm_ffn, norm_mem, final_norm, rel_bias, e_w_in, pool_w, pool_scale, mla_q_norm, mla_w_uq, mla_kv_norm, mla_w_ukv, e_w_out, o_w_in, sgu_ln_g, sgu_ln_b, sgu_w, sgu_b, swa_sinks, o_w_out, xa_wq, xa_wk, xa_wv, xa_wo, moe_w_group, moe_b_group, moe_w_expert, moe_b_expert, moe_w_gate, moe_w_up, moe_w_down):
    batch, s, d = x.shape
    assert batch == 1
    depth = norm_mix.shape[0]
    ck, sk = _rope_tables(s)
    h = x.reshape(s, d)
    mem2 = mem.reshape(mem.shape[1], d)
    norm_tile = min(512, s)
    for layer in range(depth):
        i = layer // 2
        hn = _rmsnorm(h, norm_mix[layer], BF16, tm=norm_tile)
        if layer % 2 == 0:
            h = _even_mixer(h, hn, e_w_in[i], pool_w[i], pool_scale[i], mla_q_norm[i], mla_w_uq[i],
                            mla_kv_norm[i], mla_w_ukv[i], e_w_out[i], ck, sk)
        else:
            h = _odd_mixer(h, hn, o_w_in[i], sgu_ln_g[i], sgu_ln_b[i], sgu_w[i], sgu_b[i],
                           swa_sinks[i], rel_bias, o_w_out[i])
        mem_n = _rmsnorm(mem2, norm_mem[layer], BF16, tm=mem2.shape[0])
        w_kv_mem = jnp.concatenate([xa_wk[layer], xa_wv[layer]], axis=1).astype(BF16)
        kv_mem = _matmul([mem_n], [w_kv_mem], BF16, tm=mem2.shape[0], tn=2 * XATTN_DIM,
                         name="mem_kv_proj")
        h = _xattn(h, norm_xattn[layer].reshape(1, d), xa_wq[layer].astype(BF16), kv_mem,
                   xa_wo[layer].astype(BF16), tm=min(256, s))
        h = _moe(h, norm_ffn[layer].reshape(1, d), moe_w_group[layer], moe_b_group[layer],
                 moe_w_expert[layer], moe_b_expert[layer], moe_w_gate[layer], moe_w_up[layer],
                 moe_w_down[layer])
    out = _rmsnorm(h, final_norm, F32, tm=norm_tile)
    return out.reshape(batch, s, d)
```

```python
import functools
import math

import jax
import jax.numpy as jnp
from jax import lax
from jax.experimental import pallas as pl
from jax.experimental.pallas import tpu as pltpu

F32 = jnp.float32
BF16 = jnp.bfloat16

NORM_EPS = 1e-6
NEG_INF = -1e30

POOL_WINDOWS = (2, 4, 8, 16)
POOL_GROUP_DIM = 256
POOL_DIM = 1024
POOL_HALO = 16

MLA_HEADS = 24
MLA_Q_RANK = 1024
MLA_KV_RANK = 512
MLA_NOPE_DIM = 128
MLA_ROPE_DIM = 64
MLA_V_DIM = 128
MLA_QK_DIM = MLA_NOPE_DIM + MLA_ROPE_DIM
ROPE_THETA = 10000.0

SGU_GROUPS = 8
SGU_GROUP_DIM = 256
SGU_DIM = 2048
SGU_CHUNK = 128

SWA_HEADS = 32
SWA_KV_HEADS = 8
SWA_HEAD_DIM = 64
SWA_BLOCK = 128
REL_BUCKETS = 32
REL_MAX_DIST = 128

XATTN_HEADS = 4
XATTN_HEAD_DIM = 128
XATTN_DIM = 512

N_GROUPS = 4
EXPERTS_PER_GROUP = 8
N_EXPERTS = 32
EXPERT_FF = 512
ROUTER_LANES = 128
EXPERT_TILE = 256
DMA_ISSUE_UNROLL = 8
ROW_CHUNK = 16

VMEM_LIMIT_BYTES = 56 * 1024 * 1024


def _params(*semantics):
    return pltpu.CompilerParams(dimension_semantics=semantics, vmem_limit_bytes=VMEM_LIMIT_BYTES)


def _rms(x, g):
    return x * lax.rsqrt(jnp.mean(x * x, axis=-1, keepdims=True) + NORM_EPS) * g


def _rmsnorm_kernel(x_ref, g_ref, o_ref):
    o_ref[...] = _rms(x_ref[...].astype(F32), g_ref[...]).astype(o_ref.dtype)


def _rmsnorm(x, g, out_dtype, tm):
    m, d = x.shape
    return pl.pallas_call(
        _rmsnorm_kernel,
        grid=(m // tm,),
        in_specs=[pl.BlockSpec((tm, d), lambda i: (i, 0)), pl.BlockSpec((1, d), lambda i: (0, 0))],
        out_specs=pl.BlockSpec((tm, d), lambda i: (i, 0)),
        out_shape=jax.ShapeDtypeStruct((m, d), out_dtype),
        compiler_params=_params("parallel"),
        name="rmsnorm",
    )(x, g.reshape(1, d).astype(F32))


def _matmul_kernel(*refs, n_in, has_res):
    o_ref = refs[-1]
    acc = jnp.dot(refs[0][...], refs[n_in][...], preferred_element_type=F32)
    for k in range(1, n_in):
        acc += jnp.dot(refs[k][...], refs[n_in + k][...], preferred_element_type=F32)
    if has_res:
        acc += refs[2 * n_in][...]
    o_ref[...] = acc.astype(o_ref.dtype)


def _matmul(xs, ws, out_dtype, tm, tn, residual=None, name="matmul"):
    m = xs[0].shape[0]
    n = ws[0].shape[1]
    in_specs = [pl.BlockSpec((tm, x.shape[1]), lambda i, j: (i, 0)) for x in xs]
    in_specs += [pl.BlockSpec((w.shape[0], tn), lambda i, j: (0, j)) for w in ws]
    args = list(xs) + list(ws)
    if residual is not None:
        in_specs.append(pl.BlockSpec((tm, tn), lambda i, j: (i, j)))
        args.append(residual)
    return pl.pallas_call(
        functools.partial(_matmul_kernel, n_in=len(xs), has_res=residual is not None),
        grid=(m // tm, n // tn),
        in_specs=in_specs,
        out_specs=pl.BlockSpec((tm, tn), lambda i, j: (i, j)),
        out_shape=jax.ShapeDtypeStruct((m, n), out_dtype),
        compiler_params=_params("parallel", "parallel"),
        name=name,
    )(*args)


def _pool_kernel(a_ref, halo_ref, w_ref, scale_ref, o_ref, *, ts):
    i = pl.program_id(0)
    g = pl.program_id(1)
    win = jnp.left_shift(2, g)
    a = a_ref[...]
    halo = jnp.where(i > 0, halo_ref[...], jnp.zeros_like(halo_ref))
    ext = jnp.concatenate([halo, a], axis=0)
    row = lax.broadcasted_iota(jnp.int32, (ts, POOL_HALO + ts), 0) + POOL_HALO
    col = lax.broadcasted_iota(jnp.int32, (ts, POOL_HALO + ts), 1)
    band = ((col <= row) & (col > row - win)).astype(BF16)
    wsum = jnp.dot(band, ext, preferred_element_type=F32)
    t = i * ts + lax.broadcasted_iota(jnp.int32, (ts, 1), 0)
    count = jnp.minimum(t + 1, win).astype(F32)
    d = (wsum / count - a.astype(F32)).astype(BF16)
    y = jnp.dot(d, w_ref[0], preferred_element_type=F32) * scale_ref[...]
    o_ref[...] = y.astype(o_ref.dtype)


def _pool_mixer(z, pool_w, pool_scale, ts):
    s = z.shape[0]
    c = POOL_GROUP_DIM
    hb = ts // POOL_HALO
    return pl.pallas_call(
        functools.partial(_pool_kernel, ts=ts),
        grid=(s // ts, len(POOL_WINDOWS)),
        in_specs=[
            pl.BlockSpec((ts, c), lambda i, g: (i, g)),
            pl.BlockSpec((POOL_HALO, c), lambda i, g: (jnp.maximum(i * hb - 1, 0), g)),
            pl.BlockSpec((1, c, c), lambda i, g: (g, 0, 0)),
            pl.BlockSpec((1, c), lambda i, g: (0, g)),
        ],
        out_specs=pl.BlockSpec((ts, c), lambda i, g: (i, g)),
        out_shape=jax.ShapeDtypeStruct((s, POOL_DIM), BF16),
        compiler_params=_params("parallel", "parallel"),
        name="pool_mixer",
    )(z, z, pool_w, pool_scale)


def _rope_half_block(blk, ck, sk):
    return blk * ck + pltpu.roll(blk, 64, 1) * sk


def _mla_q_kernel(cq_ref, g_ref, w_ref, ck_ref, sk_ref, o_ref, xn_ref, *, hb, scale):
    @pl.when(pl.program_id(1) == 0)
    def _():
        xn_ref[...] = _rms(cq_ref[...].astype(F32), g_ref[...]).astype(BF16)

    xn = xn_ref[...]
    for hh in range(hb):
        q = jnp.dot(xn, w_ref[hh], preferred_element_type=F32)
        o_ref[hh, :, 0:MLA_NOPE_DIM] = (q[:, :MLA_NOPE_DIM] * scale).astype(o_ref.dtype)
        pe = _rope_half_block(q[:, MLA_NOPE_DIM:], ck_ref[...], sk_ref[...]) * scale
        o_ref[hh, :, MLA_NOPE_DIM:MLA_QK_DIM] = pe[:, :MLA_ROPE_DIM].astype(o_ref.dtype)


def _mla_q(z, q_norm, w_q, ck, sk, tm, hb):
    s = z.shape[0]
    scale = MLA_QK_DIM ** -0.5 * math.log2(math.e)
    return pl.pallas_call(
        functools.partial(_mla_q_kernel, hb=hb, scale=scale),
        grid=(s // tm, MLA_HEADS // hb),
        in_specs=[
            pl.BlockSpec((tm, MLA_Q_RANK), lambda i, j: (i, POOL_DIM // MLA_Q_RANK)),
            pl.BlockSpec((1, MLA_Q_RANK), lambda i, j: (0, 0)),
            pl.BlockSpec((hb, MLA_Q_RANK, 256), lambda i, j: (j, 0, 0)),
            pl.BlockSpec((tm, 128), lambda i, j: (i, 0)),
            pl.BlockSpec((tm, 128), lambda i, j: (i, 0)),
        ],
        out_specs=pl.BlockSpec((hb, tm, MLA_QK_DIM), lambda i, j: (j, i, 0)),
        out_shape=jax.ShapeDtypeStruct((MLA_HEADS, s, MLA_QK_DIM), BF16),
        scratch_shapes=[pltpu.VMEM((tm, MLA_Q_RANK), BF16)],
        compiler_params=_params("parallel", "arbitrary"),
        name="mla_q_proj",
    )(z, q_norm, w_q, ck, sk)


def _mla_kv_kernel(ckv_ref, kpe_ref, g_ref, w_ref, ck_ref, sk_ref, k_ref, v_ref, xn_ref, kr_ref,
                   *, hb):
    @pl.when(pl.program_id(1) == 0)
    def _():
        xn_ref[...] = _rms(ckv_ref[...].astype(F32), g_ref[...]).astype(BF16)
        kr = _rope_half_block(kpe_ref[...].astype(F32), ck_ref[...], sk_ref[...])
        kr_ref[...] = kr.astype(BF16)

    xn = xn_ref[...]
    for hh in range(hb):
        kv = jnp.dot(xn, w_ref[hh], preferred_element_type=F32)
        k_ref[hh, :, 0:MLA_NOPE_DIM] = kv[:, :MLA_NOPE_DIM].astype(k_ref.dtype)
        k_ref[hh, :, MLA_NOPE_DIM:MLA_QK_DIM] = kr_ref[:, 0:MLA_ROPE_DIM]
        v_ref[hh] = kv[:, MLA_NOPE_DIM:].astype(v_ref.dtype)


def _mla_kv(z, kv_norm, w_kv, ck, sk, tm, hb):
    s = z.shape[0]
    ckv_block = (POOL_DIM + MLA_Q_RANK) // MLA_KV_RANK
    kpe_block = (POOL_DIM + MLA_Q_RANK + MLA_KV_RANK) // 128
    return pl.pallas_call(
        functools.partial(_mla_kv_kernel, hb=hb),
        grid=(s // tm, MLA_HEADS // hb),
        in_specs=[
            pl.BlockSpec((tm, MLA_KV_RANK), lambda i, j: (i, ckv_block)),
            pl.BlockSpec((tm, 128), lambda i, j: (i, kpe_block)),
            pl.BlockSpec((1, MLA_KV_RANK), lambda i, j: (0, 0)),
            pl.BlockSpec((hb, MLA_KV_RANK, 256), lambda i, j: (j, 0, 0)),
            pl.BlockSpec((tm, 128), lambda i, j: (i, 0)),
            pl.BlockSpec((tm, 128), lambda i, j: (i, 0)),
        ],
        out_specs=[
            pl.BlockSpec((hb, tm, MLA_QK_DIM), lambda i, j: (j, i, 0)),
            pl.BlockSpec((hb, tm, MLA_V_DIM), lambda i, j: (j, i, 0)),
        ],
        out_shape=[
            jax.ShapeDtypeStruct((MLA_HEADS, s, MLA_QK_DIM), BF16),
            jax.ShapeDtypeStruct((MLA_HEADS, s, MLA_V_DIM), BF16),
        ],
        scratch_shapes=[pltpu.VMEM((tm, MLA_KV_RANK), BF16), pltpu.VMEM((tm, 128), BF16)],
        compiler_params=_params("parallel", "arbitrary"),
        name="mla_kv_proj",
    )(z, z, kv_norm, w_kv, ck, sk)


def _mla_attn_kernel(q_ref, k_ref, v_ref, o_ref, m_ref, l_ref, acc_ref, s_ref, *, tb, n_sub):
    i = pl.program_id(1)
    m_ref[...] = jnp.full_like(m_ref, NEG_INF)
    l_ref[...] = jnp.zeros_like(l_ref)
    acc_ref[...] = jnp.zeros_like(acc_ref)

    def scores(x, kv):
        start = pl.multiple_of(kv * tb, tb)
        q = q_ref[0, x * tb:(x + 1) * tb, :]
        k = k_ref[0, pl.ds(start, tb), :]
        return lax.dot_general(q, k, (((1,), (1,)), ((), ())), preferred_element_type=F32)

    def update(x, kv, s, masked):
        start = pl.multiple_of(kv * tb, tb)
        v = v_ref[0, pl.ds(start, tb), :]
        if masked:
            row = lax.broadcasted_iota(jnp.int32, (tb, tb), 0)
            col = lax.broadcasted_iota(jnp.int32, (tb, tb), 1)
            s = jnp.where(col <= row, s, NEG_INF)
        m_prev = m_ref[x]
        m_new = jnp.maximum(m_prev, jnp.max(s, axis=-1, keepdims=True))
        alpha = jnp.exp2(m_prev - m_new)
        p = jnp.exp2(s - jnp.concatenate([m_new] * (tb // 128), axis=1))
        l_ref[x] = alpha * l_ref[x] + jnp.sum(p, axis=-1, keepdims=True)
        acc_ref[x] = alpha * acc_ref[x] + jnp.dot(p.astype(BF16), v, preferred_element_type=F32)
        m_ref[x] = m_new

    for x in range(n_sub):
        s_ref[x] = scores(x, 0)

    def body(kv, carry):
        for x in range(n_sub):
            s = s_ref[x]
            s_ref[x] = scores(x, kv + 1)
            update(x, kv, s, False)
        return carry

    first_diag = n_sub * i
    lax.fori_loop(0, first_diag, body, 0)
    for d in range(n_sub):
        for x in range(d, n_sub):
            s = s_ref[x] if d == 0 else scores(x, first_diag + d)
            update(x, first_diag + d, s, x == d)
    for x in range(n_sub):
        o_ref[x * tb:(x + 1) * tb, :] = (acc_ref[x] / l_ref[x]).astype(o_ref.dtype)


def _mla_attention(q, k, v, tb, n_sub):
    s = q.shape[1]
    tq = tb * n_sub
    return pl.pallas_call(
        functools.partial(_mla_attn_kernel, tb=tb, n_sub=n_sub),
        grid=(MLA_HEADS, s // tq),
        in_specs=[
            pl.BlockSpec((1, tq, MLA_QK_DIM), lambda h, i: (h, i, 0)),
            pl.BlockSpec((1, s, MLA_QK_DIM), lambda h, i: (h, 0, 0)),
            pl.BlockSpec((1, s, MLA_V_DIM), lambda h, i: (h, 0, 0)),
        ],
        out_specs=pl.BlockSpec((tq, MLA_V_DIM), lambda h, i: (i, h)),
        out_shape=jax.ShapeDtypeStruct((s, MLA_HEADS * MLA_V_DIM), BF16),
        scratch_shapes=[pltpu.VMEM((n_sub, tb, 128), F32), pltpu.VMEM((n_sub, tb, 128), F32),
                        pltpu.VMEM((n_sub, tb, MLA_V_DIM), F32),
                        pltpu.VMEM((n_sub, tb, tb), F32)],
        compiler_params=_params("parallel", "arbitrary"),
        name="mla_attention",
    )(q, k, v)


def _gelu_tanh(x):
    return 0.5 * x * (1.0 + jnp.tanh(math.sqrt(2.0 / math.pi) * (x + 0.044715 * (x * x * x))))


def _sgu_kernel(u_ref, v_ref, g_ref, b_ref, w_ref, bs_ref, o_ref, *, ts):
    row = lax.broadcasted_iota(jnp.int32, (SGU_CHUNK, SGU_CHUNK), 0)
    col = lax.broadcasted_iota(jnp.int32, (SGU_CHUNK, SGU_CHUNK), 1)
    causal = col <= row
    for c in range(ts // SGU_CHUNK):
        rows = slice(c * SGU_CHUNK, (c + 1) * SGU_CHUNK)
        v = _gelu_tanh(v_ref[rows, :].astype(F32))
        mu = jnp.mean(v, axis=-1, keepdims=True)
        vc = v - mu
        vn = vc * lax.rsqrt(jnp.mean(vc * vc, axis=-1, keepdims=True) + NORM_EPS)
        vn = (vn * g_ref[...] + b_ref[...]).astype(BF16)
        for g in range(SGU_GROUPS):
            cols = slice(g * SGU_GROUP_DIM, (g + 1) * SGU_GROUP_DIM)
            w = jnp.where(causal, w_ref[g], 0.0).astype(BF16)
            mixed = jnp.dot(w, vn[:, cols], preferred_element_type=F32) + bs_ref[:, g:g + 1]
            u = _gelu_tanh(u_ref[rows, cols].astype(F32))
            o_ref[rows, cols] = (u * mixed).astype(o_ref.dtype)


def _sgu_mixer(z, ln_g, ln_b, w_s, b_s_t, ts):
    s = z.shape[0]
    return pl.pallas_call(
        functools.partial(_sgu_kernel, ts=ts),
        grid=(s // ts,),
        in_specs=[
            pl.BlockSpec((ts, SGU_DIM), lambda i: (i, 0)),
            pl.BlockSpec((ts, SGU_DIM), lambda i: (i, 1)),
            pl.BlockSpec((1, SGU_DIM), lambda i: (0, 0)),
            pl.BlockSpec((1, SGU_DIM), lambda i: (0, 0)),
            pl.BlockSpec((SGU_GROUPS, SGU_CHUNK, SGU_CHUNK), lambda i: (0, 0, 0)),
            pl.BlockSpec((SGU_CHUNK, SGU_GROUPS), lambda i: (0, 0)),
        ],
        out_specs=pl.BlockSpec((ts, SGU_DIM), lambda i: (i, 0)),
        out_shape=jax.ShapeDtypeStruct((s, SGU_DIM), BF16),
        compiler_params=_params("parallel"),
        name="sgu_mixer",
    )(z, z, ln_g, ln_b, w_s, b_s_t)


def _swa_kernel(sink_ref, q_ref, kp_ref, kc_ref, vp_ref, vc_ref, bias_ref, o_ref):
    n = pl.program_id(0)
    blk = SWA_BLOCK
    hd = SWA_HEAD_DIM
    col = lax.broadcasted_iota(jnp.int32, (blk, 2 * blk), 1)
    no_prev = (n == 0) & (col < blk)
    q_low = lax.broadcasted_iota(jnp.int32, (blk, 2 * hd), 1) < hd
    kv_low = lax.broadcasted_iota(jnp.int32, (2 * blk, 2 * hd), 1) < hd
    scale = hd ** -0.5
    for pair in range(SWA_KV_HEADS // 2):
        cols = slice(pair * 2 * hd, (pair + 1) * 2 * hd)
        kp = jnp.concatenate([kp_ref[:, cols], kc_ref[:, cols]], axis=0).astype(F32)
        vp = jnp.concatenate([vp_ref[:, cols], vc_ref[:, cols]], axis=0).astype(F32)
        k_swapped = pltpu.roll(kp, hd, 1)
        v_swapped = pltpu.roll(vp, hd, 1)
        for half in range(2):
            kvh = 2 * pair + half
            own = kv_low if half == 0 else ~kv_low
            k_dup = jnp.where(own, kp, k_swapped).astype(BF16)
            v_dup = jnp.where(own, vp, v_swapped)
            v_side = (jnp.where(kv_low, v_dup, 0.0).astype(BF16),
                      jnp.where(kv_low, 0.0, v_dup).astype(BF16))
            for t in range(2):
                tile = 2 * kvh + t
                qt = q_ref[:, tile * 2 * hd:(tile + 1) * 2 * hd].astype(F32) * scale
                o = None
                for side in range(2):
                    h = 2 * tile + side
                    qh = jnp.where(q_low if side == 0 else ~q_low, qt, 0.0).astype(BF16)
                    s = lax.dot_general(qh, k_dup, (((1,), (1,)), ((), ())),
                                        preferred_element_type=F32)
                    s = jnp.where(no_prev, NEG_INF, s + bias_ref[h])
                    sink = sink_ref[h]
                    m = jnp.maximum(jnp.max(s, axis=-1, keepdims=True), sink)
                    p = jnp.exp(s - m)
                    denom = jnp.sum(p, axis=-1, keepdims=True) + jnp.exp(sink - m)
                    part = jnp.dot((p / denom).astype(BF16), v_side[side],
                                   preferred_element_type=F32)
                    o = part if o is None else o + part
                o_ref[:, tile * 2 * hd:(tile + 1) * 2 * hd] = o.astype(o_ref.dtype)


def _swa_mixer(z, sinks, bias):
    s = z.shape[0]
    blk = SWA_BLOCK
    qw = SWA_HEADS * SWA_HEAD_DIM
    kw = SWA_KV_HEADS * SWA_HEAD_DIM
    q_block = 2 * SGU_DIM // qw
    k_block = (2 * SGU_DIM + qw) // kw
    v_block = k_block + 1
    prev = lambda n: jnp.maximum(n - 1, 0)
    return pl.pallas_call(
        _swa_kernel,
        grid=(s // blk,),
        in_specs=[
            pl.BlockSpec(memory_space=pltpu.SMEM),
            pl.BlockSpec((blk, qw), lambda n: (n, q_block)),
            pl.BlockSpec((blk, kw), lambda n: (prev(n), k_block)),
            pl.BlockSpec((blk, kw), lambda n: (n, k_block)),
            pl.BlockSpec((blk, kw), lambda n: (prev(n), v_block)),
            pl.BlockSpec((blk, kw), lambda n: (n, v_block)),
            pl.BlockSpec((SWA_HEADS, blk, 2 * blk), lambda n: (0, 0, 0)),
        ],
        out_specs=pl.BlockSpec((blk, qw), lambda n: (n, 0)),
        out_shape=jax.ShapeDtypeStruct((s, qw), BF16),
        compiler_params=_params("parallel"),
        name="swa_mixer",
    )(sinks, z, z, z, z, z, bias)


def _t5_bucket(dist):
    max_exact = REL_BUCKETS // 2
    large = max_exact + (jnp.log(jnp.maximum(dist, 1).astype(F32) / max_exact)
                         / math.log(REL_MAX_DIST / max_exact) * (REL_BUCKETS - max_exact)).astype(jnp.int32)
    large = jnp.minimum(large, REL_BUCKETS - 1)
    return jnp.where(dist < max_exact, dist, large)


def _swa_bias(rel_bias):
    blk = SWA_BLOCK
    by_dist = rel_bias[_t5_bucket(jnp.arange(blk))].astype(F32)
    outside = jnp.full_like(by_dist, NEG_INF)
    diag = jnp.concatenate([outside, by_dist[::-1], outside], axis=0)
    period = 3 * blk
    diag = jnp.roll(diag, -(blk - 1), axis=0)
    skew = jnp.tile(diag, (blk, 1))[:blk * (period - 1)].reshape(blk, period - 1, SWA_HEADS)
    return jnp.transpose(skew[:, :2 * blk], (2, 0, 1))


def _xattn_kernel(h_ref, g_ref, wq_ref, kv_ref, wo_ref, o_ref):
    x = h_ref[...]
    xn = _rms(x, g_ref[...]).astype(BF16)
    q = jnp.dot(xn, wq_ref[...], preferred_element_type=F32) * (XATTN_HEAD_DIM ** -0.5)
    q = q.astype(BF16)
    outs = []
    for hd in range(XATTN_HEADS):
        cols = slice(hd * XATTN_HEAD_DIM, (hd + 1) * XATTN_HEAD_DIM)
        k = kv_ref[:, cols]
        v = kv_ref[:, XATTN_DIM + hd * XATTN_HEAD_DIM:XATTN_DIM + (hd + 1) * XATTN_HEAD_DIM]
        s = lax.dot_general(q[:, cols], k, (((1,), (1,)), ((), ())), preferred_element_type=F32)
        p = jnp.exp(s - jnp.max(s, axis=-1, keepdims=True))
        denom = jnp.sum(p, axis=-1, keepdims=True)
        o = jnp.dot(p.astype(BF16), v, preferred_element_type=F32) / denom
        outs.append(o.astype(BF16))
    o = jnp.concatenate(outs, axis=-1)
    o_ref[...] = x + jnp.dot(o, wo_ref[...], preferred_element_type=F32)


def _xattn(h, g, wq, kv_mem, wo, tm):
    s, d = h.shape
    mlen = kv_mem.shape[0]
    return pl.pallas_call(
        _xattn_kernel,
        grid=(s // tm,),
        in_specs=[
            pl.BlockSpec((tm, d), lambda i: (i, 0)),
            pl.BlockSpec((1, d), lambda i: (0, 0)),
            pl.BlockSpec((d, XATTN_DIM), lambda i: (0, 0)),
            pl.BlockSpec((mlen, 2 * XATTN_DIM), lambda i: (0, 0)),
            pl.BlockSpec((XATTN_DIM, d), lambda i: (0, 0)),
        ],
        out_specs=pl.BlockSpec((tm, d), lambda i: (i, 0)),
        out_shape=jax.ShapeDtypeStruct((s, d), F32),
        compiler_params=_params("parallel"),
        name="memory_xattn",
    )(h, g, wq, kv_mem, wo)


def _router_kernel(h_ref, g_ref, whi_ref, wlo_ref, b_ref, idx_ref, gate_ref, cnt_ref, carry_ref,
                   *, tm):
    @pl.when(pl.program_id(0) == 0)
    def _():
        carry_ref[...] = jnp.zeros_like(carry_ref)

    xn = _rms(h_ref[...], g_ref[...])
    x_hi = xn.astype(BF16)
    x_lo = (xn - x_hi.astype(F32)).astype(BF16)
    logits = (jnp.dot(x_hi, whi_ref[...], preferred_element_type=F32)
              + jnp.dot(x_lo, whi_ref[...], preferred_element_type=F32)
              + jnp.dot(x_hi, wlo_ref[...], preferred_element_type=F32)) + b_ref[...]

    lane = lax.broadcasted_iota(jnp.int32, (tm, ROUTER_LANES), 1)
    big = jnp.int32(ROUTER_LANES)

    def first_argmax(vals):
        top = jnp.max(vals, axis=-1, keepdims=True)
        idx = jnp.min(jnp.where(vals == top, lane, big), axis=-1, keepdims=True)
        return top, idx

    is_group = lane < N_GROUPS
    g_logits = jnp.where(is_group, logits, NEG_INF)
    g_top, g_idx = first_argmax(g_logits)
    p_group = 1.0 / jnp.sum(jnp.where(is_group, jnp.exp(g_logits - g_top), 0.0), axis=-1,
                            keepdims=True)
    lo = N_GROUPS + g_idx * EXPERTS_PER_GROUP
    in_group = (lane >= lo) & (lane < lo + EXPERTS_PER_GROUP)
    e_logits = jnp.where(in_group, logits, NEG_INF)
    top1, i1 = first_argmax(e_logits)
    top2, i2 = first_argmax(jnp.where(lane == i1, NEG_INF, e_logits))
    e21 = jnp.exp(top2 - top1)
    gate1 = p_group / (1.0 + e21)
    gate2 = p_group * e21 / (1.0 + e21)

    chosen = ((lane == i1) | (lane == i2)).astype(BF16)
    r = lax.broadcasted_iota(jnp.int32, (tm, tm), 0)
    c = lax.broadcasted_iota(jnp.int32, (tm, tm), 1)
    before = (c < r).astype(BF16)
    rank = jnp.dot(before, chosen, preferred_element_type=F32) + carry_ref[...]
    carry_ref[...] += jnp.sum(chosen.astype(F32), axis=0, keepdims=True)
    cnt_ref[...] = carry_ref[...]
    r1 = jnp.sum(jnp.where(lane == i1, rank, 0.0), axis=-1, keepdims=True).astype(jnp.int32)
    r2 = jnp.sum(jnp.where(lane == i2, rank, 0.0), axis=-1, keepdims=True).astype(jnp.int32)

    idx_ref[...] = jnp.where(lane == 0, i1 - N_GROUPS,
                             jnp.where(lane == 1, i2 - N_GROUPS,
                                       jnp.where(lane == 2, r1, jnp.where(lane == 3, r2, 0))))
    gate_ref[...] = jnp.where(lane == 0, gate1, jnp.where(lane == 1, gate2, 0.0))


def _router(h, g, w_hi, w_lo, bias, tm):
    s, d = h.shape
    row = lambda i: (i, 0)
    fixed = lambda i: (0, 0)
    return pl.pallas_call(
        functools.partial(_router_kernel, tm=tm),
        grid=(s // tm,),
        in_specs=[
            pl.BlockSpec((tm, d), row),
            pl.BlockSpec((1, d), fixed),
            pl.BlockSpec((d, ROUTER_LANES), fixed),
            pl.BlockSpec((d, ROUTER_LANES), fixed),
            pl.BlockSpec((1, ROUTER_LANES), fixed),
        ],
        out_specs=[
            pl.BlockSpec((tm, ROUTER_LANES), row),
            pl.BlockSpec((tm, ROUTER_LANES), row),
            pl.BlockSpec((1, ROUTER_LANES), fixed),
        ],
        out_shape=[
            jax.ShapeDtypeStruct((s, ROUTER_LANES), jnp.int32),
            jax.ShapeDtypeStruct((s, ROUTER_LANES), F32),
            jax.ShapeDtypeStruct((1, ROUTER_LANES), F32),
        ],
        scratch_shapes=[pltpu.VMEM((1, ROUTER_LANES), F32)],
        compiler_params=_params("arbitrary"),
        name="moe_router",
    )(h, g, w_hi, w_lo, bias)


def _row_copy(src_ref, src_row, dst_ref, dst_row, sem):
    return pltpu.make_async_copy(src_ref.at[pl.ds(src_row, 1)], dst_ref.at[pl.ds(dst_row, 1)], sem)


def _pack_bf16_pair(hi, lo):
    hi_bits = lax.bitcast_convert_type(hi.astype(BF16).astype(F32), jnp.uint32)
    lo_bits = lax.bitcast_convert_type(lo.astype(BF16).astype(F32), jnp.uint32)
    return hi_bits | (lo_bits >> 16)


def _unpack_bf16_pair(packed):
    hi = lax.bitcast_convert_type(packed & jnp.uint32(0xFFFF0000), F32)
    lo = lax.bitcast_convert_type(packed << 16, F32)
    return hi, lo


def _wait_rows(src_ref, dst_ref, sem, n):
    pltpu.make_async_copy(src_ref.at[pl.ds(0, n)], dst_ref.at[pl.ds(0, n)], sem).wait()


def _dispatch_kernel(pos_ref, h_ref, g_ref, xs_ref, buf_ref, sem, *, tb, n_steps):
    i = pl.program_id(0)
    s = tb * n_steps
    slot = lax.rem(i, 2)
    half = h_ref.shape[1] // 2

    def wait_slot(sl):
        _wait_rows(buf_ref.at[sl], xs_ref, sem.at[sl], tb)
        _wait_rows(buf_ref.at[sl], xs_ref, sem.at[sl], tb)

    @pl.when(i >= 2)
    def _():
        wait_slot(slot)

    def norm_rows(r, carry):
        rows = pl.ds(pl.multiple_of(r * ROW_CHUNK, ROW_CHUNK), ROW_CHUNK)
        xn = _rms(h_ref[rows, :], g_ref[...])
        buf_ref[slot, rows, :] = _pack_bf16_pair(xn[:, :half], xn[:, half:])
        return carry

    lax.fori_loop(0, tb // ROW_CHUNK, norm_rows, 0, unroll=4)
    base = i * tb

    def issue(t, carry):
        _row_copy(buf_ref.at[slot], t, xs_ref, pos_ref[base + t], sem.at[slot]).start()
        _row_copy(buf_ref.at[slot], t, xs_ref, pos_ref[s + base + t], sem.at[slot]).start()
        return carry

    lax.fori_loop(0, tb, issue, 0, unroll=DMA_ISSUE_UNROLL)

    @pl.when(i == n_steps - 1)
    def _():
        if n_steps > 1:
            wait_slot(1 - slot)
        wait_slot(slot)


def _dispatch(pos, h, g, tb):
    s, d = h.shape
    n_rows = pos.shape[0]
    n_steps = s // tb
    return pl.pallas_call(
        functools.partial(_dispatch_kernel, tb=tb, n_steps=n_steps),
        grid_spec=pltpu.PrefetchScalarGridSpec(
            num_scalar_prefetch=1,
            grid=(n_steps,),
            in_specs=[pl.BlockSpec((tb, d), lambda i, pos: (i, 0)),
                      pl.BlockSpec((1, d), lambda i, pos: (0, 0))],
            out_specs=pl.BlockSpec(memory_space=pl.ANY),
            scratch_shapes=[pltpu.VMEM((2, tb, d // 2), jnp.uint32),
                            pltpu.SemaphoreType.DMA((2,))],
        ),
        out_shape=jax.ShapeDtypeStruct((n_rows, d // 2), jnp.uint32),
        compiler_params=_params("arbitrary"),
        name="moe_dispatch",
    )(pos, h, g)


def _expert_kernel(tile_ref, exp_ref, lo_ref, hi_ref, first_ref, total_ref, x_ref, wg_ref, wu_ref,
                   wd_ref, o_ref):
    p = pl.program_id(0)

    @pl.when(p < total_ref[0])
    def _():
        half = wg_ref.shape[1] // 2
        x_hi, x_lo = _unpack_bf16_pair(x_ref[...])
        x_hi, x_lo = x_hi.astype(BF16), x_lo.astype(BF16)
        a = (jnp.dot(x_hi, wg_ref[0, :half, :], preferred_element_type=F32)
             + jnp.dot(x_lo, wg_ref[0, half:, :], preferred_element_type=F32))
        b = (jnp.dot(x_hi, wu_ref[0, :half, :], preferred_element_type=F32)
             + jnp.dot(x_lo, wu_ref[0, half:, :], preferred_element_type=F32))
        row = lax.broadcasted_iota(jnp.int32, (x_hi.shape[0], 1), 0)
        mine = (row >= lo_ref[p]) & (row < hi_ref[p])
        hid = (a * jax.nn.sigmoid(a) * b).astype(BF16)
        y = jnp.dot(hid, wd_ref[0], preferred_element_type=F32)
        packed = _pack_bf16_pair(y[:, :half], y[:, half:])

        @pl.when(first_ref[p] == 1)
        def _():
            o_ref[...] = packed

        @pl.when(first_ref[p] == 0)
        def _():
            o_ref[...] = jnp.where(mine, packed, o_ref[...])


def _experts(tables, xs, w_gate, w_up, w_down):
    n_rows, half = xs.shape
    d = 2 * half
    tm = EXPERT_TILE
    rows = lambda p, tile, exp, *_: (tile[p], 0)
    wsel = lambda p, tile, exp, *_: (exp[p], 0, 0)
    return pl.pallas_call(
        _expert_kernel,
        grid_spec=pltpu.PrefetchScalarGridSpec(
            num_scalar_prefetch=len(tables),
            grid=(tables[0].shape[0],),
            in_specs=[
                pl.BlockSpec((tm, half), rows),
                pl.BlockSpec((1, d, EXPERT_FF), wsel),
                pl.BlockSpec((1, d, EXPERT_FF), wsel),
                pl.BlockSpec((1, EXPERT_FF, d), wsel),
            ],
            out_specs=pl.BlockSpec((tm, half), rows),
        ),
        out_shape=jax.ShapeDtypeStruct((n_rows, half), jnp.uint32),
        compiler_params=_params("arbitrary"),
        name="moe_experts",
    )(*tables, xs, w_gate, w_up, w_down)


def _combine_kernel(pos_ref, h_ref, gate_ref, g_ref, ys_ref, *refs, tb, n_steps, final):
    if final:
        o_ref, y_ref, sem = refs
    else:
        o_ref, on_ref, y_ref, sem = refs
    i = pl.program_id(0)
    s = tb * n_steps
    slot = lax.rem(i, 2)
    half = h_ref.shape[1] // 2

    def fetch(block, sl):
        base = block * tb

        def issue(t, carry):
            _row_copy(ys_ref, pos_ref[base + t], y_ref.at[sl, 0], t, sem.at[sl]).start()
            _row_copy(ys_ref, pos_ref[s + base + t], y_ref.at[sl, 1], t, sem.at[sl]).start()
            return carry

        lax.fori_loop(0, tb, issue, 0, unroll=DMA_ISSUE_UNROLL)

    @pl.when(i == 0)
    def _():
        fetch(0, 0)

    @pl.when(i + 1 < n_steps)
    def _():
        fetch(i + 1, 1 - slot)

    _wait_rows(ys_ref, y_ref.at[slot, 0], sem.at[slot], tb)
    _wait_rows(ys_ref, y_ref.at[slot, 1], sem.at[slot], tb)
    def combine_rows(r, carry):
        rows = pl.ds(pl.multiple_of(r * ROW_CHUNK, ROW_CHUNK), ROW_CHUNK)
        gates = gate_ref[rows, :]
        g1, g2 = gates[:, 0:1], gates[:, 1:2]
        hi1, lo1 = _unpack_bf16_pair(y_ref[slot, 0, rows, :])
        hi2, lo2 = _unpack_bf16_pair(y_ref[slot, 1, rows, :])
        new_hi = h_ref[rows, :half] + g1 * hi1 + g2 * hi2
        new_lo = h_ref[rows, half:] + g1 * lo1 + g2 * lo2
        sq = (jnp.sum(new_hi * new_hi, axis=-1, keepdims=True)
              + jnp.sum(new_lo * new_lo, axis=-1, keepdims=True))
        inv = lax.rsqrt(sq / (2 * half) + NORM_EPS)
        if final:
            o_ref[rows, :half] = new_hi * inv * g_ref[:, :half]
            o_ref[rows, half:] = new_lo * inv * g_ref[:, half:]
        else:
            o_ref[rows, :half] = new_hi
            o_ref[rows, half:] = new_lo
            on_ref[rows, :half] = (new_hi * inv * g_ref[:, :half]).astype(on_ref.dtype)
            on_ref[rows, half:] = (new_lo * inv * g_ref[:, half:]).astype(on_ref.dtype)
        return carry

    lax.fori_loop(0, tb // ROW_CHUNK, combine_rows, 0, unroll=4)


def _combine(pos, h, gates, g_norm, ys, tb, final):
    s, d = h.shape
    n_steps = s // tb
    row = lambda i, pos: (i, 0)
    out_specs = [pl.BlockSpec((tb, d), row)]
    out_shape = [jax.ShapeDtypeStruct((s, d), F32)]
    if not final:
        out_specs.append(pl.BlockSpec((tb, d), row))
        out_shape.append(jax.ShapeDtypeStruct((s, d), BF16))
    return pl.pallas_call(
        functools.partial(_combine_kernel, tb=tb, n_steps=n_steps, final=final),
        grid_spec=pltpu.PrefetchScalarGridSpec(
            num_scalar_prefetch=1,
            grid=(n_steps,),
            in_specs=[
                pl.BlockSpec((tb, d), row),
                pl.BlockSpec((tb, ROUTER_LANES), row),
                pl.BlockSpec((1, d), lambda i, pos: (0, 0)),
                pl.BlockSpec(memory_space=pl.ANY),
            ],
            out_specs=out_specs,
            scratch_shapes=[pltpu.VMEM((2, 2, tb, d // 2), jnp.uint32),
                            pltpu.SemaphoreType.DMA((2,))],
        ),
        out_shape=out_shape,
        compiler_params=_params("arbitrary"),
        name="moe_combine",
    )(pos, h, gates, g_norm, ys)


def _positions_kernel(idx_ref, start_ref, o_ref):
    idx = idx_ref[...]
    lane = lax.broadcasted_iota(jnp.int32, idx.shape, 1)
    start = start_ref[...]

    def column(c):
        return jnp.sum(jnp.where(lane == c, idx, 0), axis=-1, keepdims=True)

    def seg_start(e):
        return jnp.sum(jnp.where(lane == e, start, 0), axis=-1, keepdims=True)

    pos1 = seg_start(column(0)) + column(2)
    pos2 = seg_start(column(1)) + column(3)
    o_ref[...] = jnp.where(lane == 0, pos1, jnp.where(lane == 1, pos2, 0))


def _positions(idx, seg_start, tb):
    s = idx.shape[0]
    start_row = jnp.pad(seg_start, (0, ROUTER_LANES - N_EXPERTS)).reshape(1, ROUTER_LANES)
    out = pl.pallas_call(
        _positions_kernel,
        grid=(s // tb,),
        in_specs=[pl.BlockSpec((tb, ROUTER_LANES), lambda i: (i, 0)),
                  pl.BlockSpec((1, ROUTER_LANES), lambda i: (0, 0))],
        out_specs=pl.BlockSpec((tb, ROUTER_LANES), lambda i: (i, 0)),
        out_shape=jax.ShapeDtypeStruct((s, ROUTER_LANES), jnp.int32),
        compiler_params=_params("parallel"),
        name="moe_positions",
    )(idx, start_row)
    return jnp.concatenate([out[:, 0], out[:, 1]])


def _cast_kernel(x_ref, o_ref):
    o_ref[...] = x_ref[...].astype(o_ref.dtype)


def _cast_expert_weights(w, layer):
    _, e, a, b = w.shape
    return pl.pallas_call(
        _cast_kernel,
        grid=(e,),
        in_specs=[pl.BlockSpec((None, 1, a, b), lambda i: (layer, i, 0, 0))],
        out_specs=pl.BlockSpec((1, a, b), lambda i: (i, 0, 0)),
        out_shape=jax.ShapeDtypeStruct((e, a, b), BF16),
        compiler_params=_params("parallel"),
        name="cast_expert_weights",
    )(w)


def _count_le(sorted_vals, queries):
    return jnp.sum(sorted_vals[None, :] <= queries[:, None], axis=1).astype(jnp.int32)


def _moe(h, g, g_next, final, layer, w_group, b_group, w_expert, b_expert, w_gate, w_up, w_down):
    s, d = h.shape
    pad = ROUTER_LANES - N_GROUPS - N_EXPERTS
    w_r = jnp.pad(jnp.concatenate([w_group, w_expert], axis=1), ((0, 0), (0, pad)))
    w_hi = w_r.astype(BF16)
    w_lo = (w_r - w_hi.astype(F32)).astype(BF16)
    b_r = jnp.pad(jnp.concatenate([b_group, b_expert]), (0, pad)).reshape(1, ROUTER_LANES)
    idx, gates, counts = _router(h, g, w_hi, w_lo, b_r.astype(F32), tm=min(512, s))

    tm = EXPERT_TILE
    n_tiles = 2 * s // tm
    n_pairs = n_tiles + N_EXPERTS - 1
    counts = counts[0, N_GROUPS:N_GROUPS + N_EXPERTS].astype(jnp.int32)
    seg_end = jnp.cumsum(counts)
    seg_start = seg_end - counts
    pos = _positions(idx, seg_start, tb=min(2048, s))
    tile_row = jnp.arange(n_tiles, dtype=jnp.int32) * tm
    first_e = _count_le(seg_end, tile_row)
    last_e = _count_le(seg_end, tile_row + tm - 1)
    per_tile = last_e - first_e + 1
    pair_end = jnp.cumsum(per_tile)
    pair_start = pair_end - per_tile
    total = pair_end[-1]
    p = jnp.minimum(jnp.arange(n_pairs, dtype=jnp.int32), total - 1)
    p_tile = _count_le(pair_end, p)
    p_expert = first_e[p_tile] + p - pair_start[p_tile]
    p_lo = jnp.clip(seg_start[p_expert] - p_tile * tm, 0, tm)
    p_hi = jnp.clip(seg_end[p_expert] - p_tile * tm, 0, tm)
    p_first = (p_expert == first_e[p_tile]).astype(jnp.int32)

    xs = _dispatch(pos, h, g, tb=min(256, s))
    ys = _experts((p_tile, p_expert, p_lo, p_hi, p_first, total.reshape(1)), xs,
                  _cast_expert_weights(w_gate, layer), _cast_expert_weights(w_up, layer),
                  _cast_expert_weights(w_down, layer))
    return _combine(pos, h, gates, g_next, ys, tb=min(256, s), final=final)


def _rope_tables(s):
    inv_freq = ROPE_THETA ** (-jnp.arange(0, MLA_ROPE_DIM, 2, dtype=F32) / MLA_ROPE_DIM)
    ang = jnp.arange(s, dtype=F32)[:, None] * inv_freq[None, :]
    cos, sin = jnp.cos(ang), jnp.sin(ang)
    zeros = jnp.zeros((s, 128 - MLA_ROPE_DIM), F32)
    return (jnp.concatenate([cos, cos, zeros], axis=1), jnp.concatenate([sin, sin, zeros], axis=1))


def _rot_half_cols(w):
    half = MLA_ROPE_DIM // 2
    return jnp.concatenate([-w[..., half:], w[..., :half]], axis=-1)


def _even_mixer(h, hn, w_in, pool_w, pool_scale, q_norm, w_uq, kv_norm, w_ukv, w_out, ck, sk):
    s, d = h.shape
    w_kpe = w_in[:, POOL_DIM + MLA_Q_RANK + MLA_KV_RANK:]
    w_in_ext = jnp.concatenate([w_in, _rot_half_cols(w_kpe)], axis=1).astype(BF16)
    big = min(1024, s)
    z = _matmul([hn], [w_in_ext], BF16, tm=big, tn=896, name="even_in_proj")

    ya = _pool_mixer(z, pool_w.astype(BF16), pool_scale.reshape(1, POOL_DIM), ts=min(512, s))

    w_q = w_uq.reshape(MLA_Q_RANK, MLA_HEADS, MLA_QK_DIM)
    w_q = jnp.concatenate([w_q, _rot_half_cols(w_q[..., MLA_NOPE_DIM:])], axis=-1)
    w_q = jnp.transpose(w_q, (1, 0, 2)).astype(BF16)
    w_kv = jnp.transpose(w_ukv.reshape(MLA_KV_RANK, MLA_HEADS, MLA_NOPE_DIM + MLA_V_DIM),
                         (1, 0, 2)).astype(BF16)
    q = _mla_q(z, q_norm.reshape(1, -1), w_q, ck, sk, tm=big, hb=4)
    k, v = _mla_kv(z, kv_norm.reshape(1, -1), w_kv, ck, sk, tm=big, hb=4)
    yb = _mla_attention(q, k, v, tb=min(512, s), n_sub=4 if s >= 2048 else 1)

    w_o = w_out.astype(BF16)
    return _matmul([ya, yb], [w_o[:POOL_DIM], w_o[POOL_DIM:]], F32, tm=min(512, s), tn=1024,
                   residual=h, name="even_out_proj")


def _odd_mixer(h, hn, w_in, ln_g, ln_b, w_s, b_s, sinks, rel_bias, w_out):
    s, d = h.shape
    big = min(1024, s)
    z = _matmul([hn], [w_in.astype(BF16)], BF16, tm=big, tn=1024, name="odd_in_proj")
    yc = _sgu_mixer(z, ln_g.reshape(1, -1), ln_b.reshape(1, -1), w_s, jnp.transpose(b_s),
                    ts=min(256, s))
    yd = _swa_mixer(z, sinks, _swa_bias(rel_bias))
    w_o = w_out.astype(BF16)
    return _matmul([yc, yd], [w_o[:SGU_DIM], w_o[SGU_DIM:]], F32, tm=min(512, s), tn=1024,
                   residual=h, name="odd_out_proj")


def kernel(x, mem, norm_mix, norm_xattn, norm_ffn, norm_mem, final_norm, rel_bias, e_w_in, pool_w, pool_scale, mla_q_norm, mla_w_uq, mla_kv_norm, mla_w_ukv, e_w_out, o_w_in, sgu_ln_g, sgu_ln_b, sgu_w, sgu_b, swa_sinks, o_w_out, xa_wq, xa_wk, xa_wv, xa_wo, moe_w_group, moe_b_group, moe_w_expert, moe_b_expert, moe_w_gate, moe_w_up, moe_w_down):
    batch, s, d = x.shape
    assert batch == 1
    depth = norm_mix.shape[0]
    ck, sk = _rope_tables(s)
    h = x.reshape(s, d)
    mem2 = mem.reshape(mem.shape[1], d)
    hn = _rmsnorm(h, norm_mix[0], BF16, tm=min(512, s))
    for layer in range(depth):
        i = layer // 2
        last = layer == depth - 1
        if layer % 2 == 0:
            h = _even_mixer(h, hn, e_w_in[i], pool_w[i], pool_scale[i], mla_q_norm[i], mla_w_uq[i],
                            mla_kv_norm[i], mla_w_ukv[i], e_w_out[i], ck, sk)
        else:
            h = _odd_mixer(h, hn, o_w_in[i], sgu_ln_g[i], sgu_ln_b[i], sgu_w[i], sgu_b[i],
                           swa_sinks[i], rel_bias, o_w_out[i])
        mem_n = _rmsnorm(mem2, norm_mem[layer], BF16, tm=mem2.shape[0])
        w_kv_mem = jnp.concatenate([xa_wk[layer], xa_wv[layer]], axis=1).astype(BF16)
        kv_mem = _matmul([mem_n], [w_kv_mem], BF16, tm=mem2.shape[0], tn=2 * XATTN_DIM,
                         name="mem_kv_proj")
        h = _xattn(h, norm_xattn[layer].reshape(1, d), xa_wq[layer].astype(BF16), kv_mem,
                   xa_wo[layer].astype(BF16), tm=min(256, s))
        g_next = final_norm if last else norm_mix[layer + 1]
        res = _moe(h, norm_ffn[layer].reshape(1, d), g_next.reshape(1, d), last, layer,
                   moe_w_group[layer], moe_b_group[layer], moe_w_expert[layer],
                   moe_b_expert[layer], moe_w_gate, moe_w_up, moe_w_down)
        if last:
            out, = res
        else:
            h, hn = res
    return out.reshape(batch, s, d)
```

```python
import functools
import math

import jax
import jax.numpy as jnp
from jax import lax
from jax.experimental import pallas as pl
from jax.experimental.pallas import tpu as pltpu

F32 = jnp.float32
BF16 = jnp.bfloat16

NORM_EPS = 1e-6
NEG_INF = -1e30

POOL_WINDOWS = (2, 4, 8, 16)
POOL_GROUP_DIM = 256
POOL_DIM = 1024
POOL_HALO = 16

MLA_HEADS = 24
MLA_Q_RANK = 1024
MLA_KV_RANK = 512
MLA_NOPE_DIM = 128
MLA_ROPE_DIM = 64
MLA_V_DIM = 128
MLA_QK_DIM = MLA_NOPE_DIM + MLA_ROPE_DIM
ROPE_THETA = 10000.0

SGU_GROUPS = 8
SGU_GROUP_DIM = 256
SGU_DIM = 2048
SGU_CHUNK = 128

SWA_HEADS = 32
SWA_KV_HEADS = 8
SWA_HEAD_DIM = 64
SWA_BLOCK = 128
REL_BUCKETS = 32
REL_MAX_DIST = 128

XATTN_HEADS = 4
XATTN_HEAD_DIM = 128
XATTN_DIM = 512

N_GROUPS = 4
EXPERTS_PER_GROUP = 8
N_EXPERTS = 32
EXPERT_FF = 512
ROUTER_LANES = 128
EXPERT_TILE = 256
DMA_ISSUE_UNROLL = 8
ROW_CHUNK = 16

VMEM_LIMIT_BYTES = 56 * 1024 * 1024


def _params(*semantics):
    return pltpu.CompilerParams(dimension_semantics=semantics, vmem_limit_bytes=VMEM_LIMIT_BYTES)


def _rms(x, g):
    return x * lax.rsqrt(jnp.mean(x * x, axis=-1, keepdims=True) + NORM_EPS) * g


def _rmsnorm_kernel(x_ref, g_ref, o_ref):
    o_ref[...] = _rms(x_ref[...].astype(F32), g_ref[...]).astype(o_ref.dtype)


def _rmsnorm(x, g, out_dtype, tm):
    m, d = x.shape
    return pl.pallas_call(
        _rmsnorm_kernel,
        grid=(m // tm,),
        in_specs=[pl.BlockSpec((tm, d), lambda i: (i, 0)), pl.BlockSpec((1, d), lambda i: (0, 0))],
        out_specs=pl.BlockSpec((tm, d), lambda i: (i, 0)),
        out_shape=jax.ShapeDtypeStruct((m, d), out_dtype),
        compiler_params=_params("parallel"),
        name="rmsnorm",
    )(x, g.reshape(1, d).astype(F32))


def _matmul_kernel(*refs, n_in, has_res):
    o_ref = refs[-1]
    acc = jnp.dot(refs[0][...], refs[n_in][...], preferred_element_type=F32)
    for k in range(1, n_in):
        acc += jnp.dot(refs[k][...], refs[n_in + k][...], preferred_element_type=F32)
    if has_res:
        acc += refs[2 * n_in][...]
    o_ref[...] = acc.astype(o_ref.dtype)


def _matmul(xs, ws, out_dtype, tm, tn, residual=None, name="matmul"):
    m = xs[0].shape[0]
    n = ws[0].shape[1]
    in_specs = [pl.BlockSpec((tm, x.shape[1]), lambda i, j: (i, 0)) for x in xs]
    in_specs += [pl.BlockSpec((w.shape[0], tn), lambda i, j: (0, j)) for w in ws]
    args = list(xs) + list(ws)
    if residual is not None:
        in_specs.append(pl.BlockSpec((tm, tn), lambda i, j: (i, j)))
        args.append(residual)
    return pl.pallas_call(
        functools.partial(_matmul_kernel, n_in=len(xs), has_res=residual is not None),
        grid=(m // tm, n // tn),
        in_specs=in_specs,
        out_specs=pl.BlockSpec((tm, tn), lambda i, j: (i, j)),
        out_shape=jax.ShapeDtypeStruct((m, n), out_dtype),
        compiler_params=_params("parallel", "parallel"),
        name=name,
    )(*args)


def _pool_kernel(a_ref, halo_ref, w_ref, scale_ref, o_ref, *, ts):
    i = pl.program_id(0)
    g = pl.program_id(1)
    win = jnp.left_shift(2, g)
    a = a_ref[...]
    halo = jnp.where(i > 0, halo_ref[...], jnp.zeros_like(halo_ref))
    ext = jnp.concatenate([halo, a], axis=0)
    row = lax.broadcasted_iota(jnp.int32, (ts, POOL_HALO + ts), 0) + POOL_HALO
    col = lax.broadcasted_iota(jnp.int32, (ts, POOL_HALO + ts), 1)
    band = ((col <= row) & (col > row - win)).astype(BF16)
    wsum = jnp.dot(band, ext, preferred_element_type=F32)
    t = i * ts + lax.broadcasted_iota(jnp.int32, (ts, 1), 0)
    count = jnp.minimum(t + 1, win).astype(F32)
    d = (wsum / count - a.astype(F32)).astype(BF16)
    y = jnp.dot(d, w_ref[0], preferred_element_type=F32) * scale_ref[...]
    o_ref[...] = y.astype(o_ref.dtype)


def _pool_mixer(z, pool_w, pool_scale, ts):
    s = z.shape[0]
    c = POOL_GROUP_DIM
    hb = ts // POOL_HALO
    return pl.pallas_call(
        functools.partial(_pool_kernel, ts=ts),
        grid=(s // ts, len(POOL_WINDOWS)),
        in_specs=[
            pl.BlockSpec((ts, c), lambda i, g: (i, g)),
            pl.BlockSpec((POOL_HALO, c), lambda i, g: (jnp.maximum(i * hb - 1, 0), g)),
            pl.BlockSpec((1, c, c), lambda i, g: (g, 0, 0)),
            pl.BlockSpec((1, c), lambda i, g: (0, g)),
        ],
        out_specs=pl.BlockSpec((ts, c), lambda i, g: (i, g)),
        out_shape=jax.ShapeDtypeStruct((s, POOL_DIM), BF16),
        compiler_params=_params("parallel", "parallel"),
        name="pool_mixer",
    )(z, z, pool_w, pool_scale)


def _rope_half_block(blk, ck, sk):
    return blk * ck + pltpu.roll(blk, 64, 1) * sk


def _mla_q_kernel(cq_ref, g_ref, w_ref, ck_ref, sk_ref, o_ref, xn_ref, *, hb, scale):
    @pl.when(pl.program_id(1) == 0)
    def _():
        xn_ref[...] = _rms(cq_ref[...].astype(F32), g_ref[...]).astype(BF16)

    xn = xn_ref[...]
    for hh in range(hb):
        q = jnp.dot(xn, w_ref[hh], preferred_element_type=F32)
        o_ref[hh, :, 0:MLA_NOPE_DIM] = (q[:, :MLA_NOPE_DIM] * scale).astype(o_ref.dtype)
        pe = _rope_half_block(q[:, MLA_NOPE_DIM:], ck_ref[...], sk_ref[...]) * scale
        o_ref[hh, :, MLA_NOPE_DIM:MLA_QK_DIM] = pe[:, :MLA_ROPE_DIM].astype(o_ref.dtype)


def _mla_q(z, q_norm, w_q, ck, sk, tm, hb):
    s = z.shape[0]
    scale = MLA_QK_DIM ** -0.5 * math.log2(math.e)
    return pl.pallas_call(
        functools.partial(_mla_q_kernel, hb=hb, scale=scale),
        grid=(s // tm, MLA_HEADS // hb),
        in_specs=[
            pl.BlockSpec((tm, MLA_Q_RANK), lambda i, j: (i, POOL_DIM // MLA_Q_RANK)),
            pl.BlockSpec((1, MLA_Q_RANK), lambda i, j: (0, 0)),
            pl.BlockSpec((hb, MLA_Q_RANK, 256), lambda i, j: (j, 0, 0)),
            pl.BlockSpec((tm, 128), lambda i, j: (i, 0)),
            pl.BlockSpec((tm, 128), lambda i, j: (i, 0)),
        ],
        out_specs=pl.BlockSpec((hb, tm, MLA_QK_DIM), lambda i, j: (j, i, 0)),
        out_shape=jax.ShapeDtypeStruct((MLA_HEADS, s, MLA_QK_DIM), BF16),
        scratch_shapes=[pltpu.VMEM((tm, MLA_Q_RANK), BF16)],
        compiler_params=_params("parallel", "arbitrary"),
        name="mla_q_proj",
    )(z, q_norm, w_q, ck, sk)


def _mla_kv_kernel(ckv_ref, kpe_ref, g_ref, wk_ref, wvt_ref, ck_ref, sk_ref, k_ref, vt_ref,
                   xn_ref, kr_ref, *, hb):
    @pl.when(pl.program_id(1) == 0)
    def _():
        xn_ref[...] = _rms(ckv_ref[...].astype(F32), g_ref[...]).astype(BF16)
        kr = _rope_half_block(kpe_ref[...].astype(F32), ck_ref[...], sk_ref[...])
        kr_ref[...] = kr.astype(BF16)

    xn = xn_ref[...]
    for pair in range(hb // 2):
        kk = jnp.dot(xn, wk_ref[pair], preferred_element_type=F32)
        vv = lax.dot_general(wvt_ref[pair], xn, (((1,), (1,)), ((), ())),
                             preferred_element_type=F32)
        for side in range(2):
            hh = 2 * pair + side
            k_ref[hh, :, 0:MLA_NOPE_DIM] = (
                kk[:, side * MLA_NOPE_DIM:(side + 1) * MLA_NOPE_DIM].astype(k_ref.dtype))
            k_ref[hh, :, MLA_NOPE_DIM:MLA_QK_DIM] = kr_ref[:, 0:MLA_ROPE_DIM]
            vt_ref[hh] = vv[side * MLA_V_DIM:(side + 1) * MLA_V_DIM, :].astype(vt_ref.dtype)


def _mla_kv(z, kv_norm, w_k, w_vt, ck, sk, tm, hb):
    s = z.shape[0]
    ckv_block = (POOL_DIM + MLA_Q_RANK) // MLA_KV_RANK
    kpe_block = (POOL_DIM + MLA_Q_RANK + MLA_KV_RANK) // 128
    return pl.pallas_call(
        functools.partial(_mla_kv_kernel, hb=hb),
        grid=(s // tm, MLA_HEADS // hb),
        in_specs=[
            pl.BlockSpec((tm, MLA_KV_RANK), lambda i, j: (i, ckv_block)),
            pl.BlockSpec((tm, 128), lambda i, j: (i, kpe_block)),
            pl.BlockSpec((1, MLA_KV_RANK), lambda i, j: (0, 0)),
            pl.BlockSpec((hb // 2, MLA_KV_RANK, 2 * MLA_NOPE_DIM), lambda i, j: (j, 0, 0)),
            pl.BlockSpec((hb // 2, 2 * MLA_V_DIM, MLA_KV_RANK), lambda i, j: (j, 0, 0)),
            pl.BlockSpec((tm, 128), lambda i, j: (i, 0)),
            pl.BlockSpec((tm, 128), lambda i, j: (i, 0)),
        ],
        out_specs=[
            pl.BlockSpec((hb, tm, MLA_QK_DIM), lambda i, j: (j, i, 0)),
            pl.BlockSpec((hb, MLA_V_DIM, tm), lambda i, j: (j, 0, i)),
        ],
        out_shape=[
            jax.ShapeDtypeStruct((MLA_HEADS, s, MLA_QK_DIM), BF16),
            jax.ShapeDtypeStruct((MLA_HEADS, MLA_V_DIM, s), BF16),
        ],
        scratch_shapes=[pltpu.VMEM((tm, MLA_KV_RANK), BF16), pltpu.VMEM((tm, 128), BF16)],
        compiler_params=_params("parallel", "arbitrary"),
        name="mla_kv_proj",
    )(z, z, kv_norm, w_k, w_vt, ck, sk)


def _mla_attn_kernel(q_ref, k_ref, v_ref, o_ref, m_ref, l_ref, acc_ref, s_ref, *, tb, n_sub):
    i = pl.program_id(1)
    m_ref[...] = jnp.full_like(m_ref, NEG_INF)
    l_ref[...] = jnp.zeros_like(l_ref)
    acc_ref[...] = jnp.zeros_like(acc_ref)

    def scores(x, kv):
        start = pl.multiple_of(kv * tb, tb)
        q = q_ref[0, x * tb:(x + 1) * tb, :]
        k = k_ref[0, pl.ds(start, tb), :]
        return lax.dot_general(k, q, (((1,), (1,)), ((), ())), preferred_element_type=F32)

    def update(x, kv, s, masked):
        start = pl.multiple_of(kv * tb, tb)
        vt = v_ref[0, :, pl.ds(start, tb)]
        if masked:
            key = lax.broadcasted_iota(jnp.int32, (tb, tb), 0)
            qry = lax.broadcasted_iota(jnp.int32, (tb, tb), 1)
            s = jnp.where(key <= qry, s, NEG_INF)
        m_prev = m_ref[x]
        m_new = jnp.maximum(m_prev, jnp.max(s, axis=0, keepdims=True))
        alpha = jnp.exp2(m_prev - m_new)
        p = jnp.exp2(s - m_new)
        l_ref[x] = alpha * l_ref[x] + jnp.sum(p, axis=0, keepdims=True)
        acc_ref[x] = alpha * acc_ref[x] + jnp.dot(vt, p.astype(BF16), preferred_element_type=F32)
        m_ref[x] = m_new

    for x in range(n_sub):
        s_ref[x] = scores(x, 0)

    def body(kv, carry):
        for x in range(n_sub):
            s = s_ref[x]
            s_ref[x] = scores(x, kv + 1)
            update(x, kv, s, False)
        return carry

    first_diag = n_sub * i
    lax.fori_loop(0, first_diag, body, 0)
    for d in range(n_sub):
        for x in range(d, n_sub):
            s = s_ref[x] if d == 0 else scores(x, first_diag + d)
            update(x, first_diag + d, s, x == d)
    for x in range(n_sub):
        o_ref[x * tb:(x + 1) * tb, :] = jnp.transpose(acc_ref[x] / l_ref[x]).astype(o_ref.dtype)


def _mla_attention(q, k, v, tb, n_sub):
    s = q.shape[1]
    tq = tb * n_sub
    return pl.pallas_call(
        functools.partial(_mla_attn_kernel, tb=tb, n_sub=n_sub),
        grid=(MLA_HEADS, s // tq),
        in_specs=[
            pl.BlockSpec((1, tq, MLA_QK_DIM), lambda h, i: (h, i, 0)),
            pl.BlockSpec((1, s, MLA_QK_DIM), lambda h, i: (h, 0, 0)),
            pl.BlockSpec((1, MLA_V_DIM, s), lambda h, i: (h, 0, 0)),
        ],
        out_specs=pl.BlockSpec((tq, MLA_V_DIM), lambda h, i: (i, h)),
        out_shape=jax.ShapeDtypeStruct((s, MLA_HEADS * MLA_V_DIM), BF16),
        scratch_shapes=[pltpu.VMEM((n_sub, 1, tb), F32), pltpu.VMEM((n_sub, 1, tb), F32),
                        pltpu.VMEM((n_sub, MLA_V_DIM, tb), F32),
                        pltpu.VMEM((n_sub, tb, tb), F32)],
        compiler_params=_params("parallel", "arbitrary"),
        name="mla_attention",
    )(q, k, v)


def _gelu_tanh(x):
    return 0.5 * x * (1.0 + jnp.tanh(math.sqrt(2.0 / math.pi) * (x + 0.044715 * (x * x * x))))


def _sgu_kernel(u_ref, v_ref, g_ref, b_ref, w_ref, bs_ref, o_ref, *, ts):
    row = lax.broadcasted_iota(jnp.int32, (SGU_CHUNK, SGU_CHUNK), 0)
    col = lax.broadcasted_iota(jnp.int32, (SGU_CHUNK, SGU_CHUNK), 1)
    causal = col <= row
    for c in range(ts // SGU_CHUNK):
        rows = slice(c * SGU_CHUNK, (c + 1) * SGU_CHUNK)
        v = _gelu_tanh(v_ref[rows, :].astype(F32))
        mu = jnp.mean(v, axis=-1, keepdims=True)
        vc = v - mu
        vn = vc * lax.rsqrt(jnp.mean(vc * vc, axis=-1, keepdims=True) + NORM_EPS)
        vn = (vn * g_ref[...] + b_ref[...]).astype(BF16)
        for g in range(SGU_GROUPS):
            cols = slice(g * SGU_GROUP_DIM, (g + 1) * SGU_GROUP_DIM)
            w = jnp.where(causal, w_ref[g], 0.0).astype(BF16)
            mixed = jnp.dot(w, vn[:, cols], preferred_element_type=F32) + bs_ref[:, g:g + 1]
            u = _gelu_tanh(u_ref[rows, cols].astype(F32))
            o_ref[rows, cols] = (u * mixed).astype(o_ref.dtype)


def _sgu_mixer(z, ln_g, ln_b, w_s, b_s_t, ts):
    s = z.shape[0]
    return pl.pallas_call(
        functools.partial(_sgu_kernel, ts=ts),
        grid=(s // ts,),
        in_specs=[
            pl.BlockSpec((ts, SGU_DIM), lambda i: (i, 0)),
            pl.BlockSpec((ts, SGU_DIM), lambda i: (i, 1)),
            pl.BlockSpec((1, SGU_DIM), lambda i: (0, 0)),
            pl.BlockSpec((1, SGU_DIM), lambda i: (0, 0)),
            pl.BlockSpec((SGU_GROUPS, SGU_CHUNK, SGU_CHUNK), lambda i: (0, 0, 0)),
            pl.BlockSpec((SGU_CHUNK, SGU_GROUPS), lambda i: (0, 0)),
        ],
        out_specs=pl.BlockSpec((ts, SGU_DIM), lambda i: (i, 0)),
        out_shape=jax.ShapeDtypeStruct((s, SGU_DIM), BF16),
        compiler_params=_params("parallel"),
        name="sgu_mixer",
    )(z, z, ln_g, ln_b, w_s, b_s_t)


def _swa_kernel(sink_ref, q_ref, kp_ref, kc_ref, vp_ref, vc_ref, bias_ref, o_ref):
    n = pl.program_id(0)
    blk = SWA_BLOCK
    hd = SWA_HEAD_DIM
    col = lax.broadcasted_iota(jnp.int32, (blk, 2 * blk), 1)
    no_prev = (n == 0) & (col < blk)
    q_low = lax.broadcasted_iota(jnp.int32, (blk, 2 * hd), 1) < hd
    kv_low = lax.broadcasted_iota(jnp.int32, (2 * blk, 2 * hd), 1) < hd
    scale = hd ** -0.5
    for pair in range(SWA_KV_HEADS // 2):
        cols = slice(pair * 2 * hd, (pair + 1) * 2 * hd)
        kp = jnp.concatenate([kp_ref[:, cols], kc_ref[:, cols]], axis=0).astype(F32)
        vp = jnp.concatenate([vp_ref[:, cols], vc_ref[:, cols]], axis=0).astype(F32)
        k_swapped = pltpu.roll(kp, hd, 1)
        v_swapped = pltpu.roll(vp, hd, 1)
        for half in range(2):
            kvh = 2 * pair + half
            own = kv_low if half == 0 else ~kv_low
            k_dup = jnp.where(own, kp, k_swapped).astype(BF16)
            v_dup = jnp.where(own, vp, v_swapped)
            v_side = (jnp.where(kv_low, v_dup, 0.0).astype(BF16),
                      jnp.where(kv_low, 0.0, v_dup).astype(BF16))
            for t in range(2):
                tile = 2 * kvh + t
                qt = q_ref[:, tile * 2 * hd:(tile + 1) * 2 * hd].astype(F32) * scale
                o = None
                for side in range(2):
                    h = 2 * tile + side
                    qh = jnp.where(q_low if side == 0 else ~q_low, qt, 0.0).astype(BF16)
                    s = lax.dot_general(qh, k_dup, (((1,), (1,)), ((), ())),
                                        preferred_element_type=F32)
                    s = jnp.where(no_prev, NEG_INF, s + bias_ref[h])
                    sink = sink_ref[h]
                    m = jnp.maximum(jnp.max(s, axis=-1, keepdims=True), sink)
                    p = jnp.exp(s - m)
                    denom = jnp.sum(p, axis=-1, keepdims=True) + jnp.exp(sink - m)
                    part = jnp.dot(p.astype(BF16), v_side[side],
                                   preferred_element_type=F32) / denom
                    o = part if o is None else o + part
                o_ref[:, tile * 2 * hd:(tile + 1) * 2 * hd] = o.astype(o_ref.dtype)


def _swa_mixer(z, sinks, bias):
    s = z.shape[0]
    blk = SWA_BLOCK
    qw = SWA_HEADS * SWA_HEAD_DIM
    kw = SWA_KV_HEADS * SWA_HEAD_DIM
    q_block = 2 * SGU_DIM // qw
    k_block = (2 * SGU_DIM + qw) // kw
    v_block = k_block + 1
    prev = lambda n: jnp.maximum(n - 1, 0)
    return pl.pallas_call(
        _swa_kernel,
        grid=(s // blk,),
        in_specs=[
            pl.BlockSpec(memory_space=pltpu.SMEM),
            pl.BlockSpec((blk, qw), lambda n: (n, q_block)),
            pl.BlockSpec((blk, kw), lambda n: (prev(n), k_block)),
            pl.BlockSpec((blk, kw), lambda n: (n, k_block)),
            pl.BlockSpec((blk, kw), lambda n: (prev(n), v_block)),
            pl.BlockSpec((blk, kw), lambda n: (n, v_block)),
            pl.BlockSpec((SWA_HEADS, blk, 2 * blk), lambda n: (0, 0, 0)),
        ],
        out_specs=pl.BlockSpec((blk, qw), lambda n: (n, 0)),
        out_shape=jax.ShapeDtypeStruct((s, qw), BF16),
        compiler_params=_params("parallel"),
        name="swa_mixer",
    )(sinks, z, z, z, z, z, bias)


def _t5_bucket(dist):
    max_exact = REL_BUCKETS // 2
    large = max_exact + (jnp.log(jnp.maximum(dist, 1).astype(F32) / max_exact)
                         / math.log(REL_MAX_DIST / max_exact) * (REL_BUCKETS - max_exact)).astype(jnp.int32)
    large = jnp.minimum(large, REL_BUCKETS - 1)
    return jnp.where(dist < max_exact, dist, large)


def _swa_bias(rel_bias):
    blk = SWA_BLOCK
    by_dist = rel_bias[_t5_bucket(jnp.arange(blk))].astype(F32)
    outside = jnp.full_like(by_dist, NEG_INF)
    diag = jnp.concatenate([outside, by_dist[::-1], outside], axis=0)
    period = 3 * blk
    diag = jnp.roll(diag, -(blk - 1), axis=0)
    skew = jnp.tile(diag, (blk, 1))[:blk * (period - 1)].reshape(blk, period - 1, SWA_HEADS)
    return jnp.transpose(skew[:, :2 * blk], (2, 0, 1))


def _xattn_kernel(h_ref, g_ref, wq_ref, kv_ref, wo_ref, o_ref):
    x = h_ref[...]
    xn = _rms(x, g_ref[...]).astype(BF16)
    q = jnp.dot(xn, wq_ref[...], preferred_element_type=F32) * (XATTN_HEAD_DIM ** -0.5)
    q = q.astype(BF16)
    outs = []
    for hd in range(XATTN_HEADS):
        cols = slice(hd * XATTN_HEAD_DIM, (hd + 1) * XATTN_HEAD_DIM)
        k = kv_ref[:, cols]
        v = kv_ref[:, XATTN_DIM + hd * XATTN_HEAD_DIM:XATTN_DIM + (hd + 1) * XATTN_HEAD_DIM]
        s = lax.dot_general(q[:, cols], k, (((1,), (1,)), ((), ())), preferred_element_type=F32)
        p = jnp.exp(s - jnp.max(s, axis=-1, keepdims=True))
        denom = jnp.sum(p, axis=-1, keepdims=True)
        o = jnp.dot(p.astype(BF16), v, preferred_element_type=F32) / denom
        outs.append(o.astype(BF16))
    o = jnp.concatenate(outs, axis=-1)
    o_ref[...] = x + jnp.dot(o, wo_ref[...], preferred_element_type=F32)


def _xattn(h, g, wq, kv_mem, wo, tm):
    s, d = h.shape
    mlen = kv_mem.shape[0]
    return pl.pallas_call(
        _xattn_kernel,
        grid=(s // tm,),
        in_specs=[
            pl.BlockSpec((tm, d), lambda i: (i, 0)),
            pl.BlockSpec((1, d), lambda i: (0, 0)),
            pl.BlockSpec((d, XATTN_DIM), lambda i: (0, 0)),
            pl.BlockSpec((mlen, 2 * XATTN_DIM), lambda i: (0, 0)),
            pl.BlockSpec((XATTN_DIM, d), lambda i: (0, 0)),
        ],
        out_specs=pl.BlockSpec((tm, d), lambda i: (i, 0)),
        out_shape=jax.ShapeDtypeStruct((s, d), F32),
        compiler_params=_params("parallel"),
        name="memory_xattn",
    )(h, g, wq, kv_mem, wo)


def _router_kernel(h_ref, g_ref, whi_ref, wlo_ref, b_ref, idx_ref, gate_ref, cnt_ref, carry_ref,
                   *, tm):
    @pl.when(pl.program_id(0) == 0)
    def _():
        carry_ref[...] = jnp.zeros_like(carry_ref)

    xn = _rms(h_ref[...], g_ref[...])
    x_hi = xn.astype(BF16)
    x_lo = (xn - x_hi.astype(F32)).astype(BF16)
    logits = (jnp.dot(x_hi, whi_ref[...], preferred_element_type=F32)
              + jnp.dot(x_lo, whi_ref[...], preferred_element_type=F32)
              + jnp.dot(x_hi, wlo_ref[...], preferred_element_type=F32)) + b_ref[...]

    lane = lax.broadcasted_iota(jnp.int32, (tm, ROUTER_LANES), 1)
    big = jnp.int32(ROUTER_LANES)

    def first_argmax(vals):
        top = jnp.max(vals, axis=-1, keepdims=True)
        idx = jnp.min(jnp.where(vals == top, lane, big), axis=-1, keepdims=True)
        return top, idx

    is_group = lane < N_GROUPS
    g_logits = jnp.where(is_group, logits, NEG_INF)
    g_top, g_idx = first_argmax(g_logits)
    p_group = 1.0 / jnp.sum(jnp.where(is_group, jnp.exp(g_logits - g_top), 0.0), axis=-1,
                            keepdims=True)
    lo = N_GROUPS + g_idx * EXPERTS_PER_GROUP
    in_group = (lane >= lo) & (lane < lo + EXPERTS_PER_GROUP)
    e_logits = jnp.where(in_group, logits, NEG_INF)
    top1, i1 = first_argmax(e_logits)
    top2, i2 = first_argmax(jnp.where(lane == i1, NEG_INF, e_logits))
    e21 = jnp.exp(top2 - top1)
    gate1 = p_group / (1.0 + e21)
    gate2 = p_group * e21 / (1.0 + e21)

    chosen = ((lane == i1) | (lane == i2)).astype(BF16)
    r = lax.broadcasted_iota(jnp.int32, (tm, tm), 0)
    c = lax.broadcasted_iota(jnp.int32, (tm, tm), 1)
    before = (c < r).astype(BF16)
    rank = jnp.dot(before, chosen, preferred_element_type=F32) + carry_ref[...]
    carry_ref[...] += jnp.sum(chosen.astype(F32), axis=0, keepdims=True)
    cnt_ref[...] = carry_ref[...]
    r1 = jnp.sum(jnp.where(lane == i1, rank, 0.0), axis=-1, keepdims=True).astype(jnp.int32)
    r2 = jnp.sum(jnp.where(lane == i2, rank, 0.0), axis=-1, keepdims=True).astype(jnp.int32)

    idx_ref[...] = jnp.where(lane == 0, i1 - N_GROUPS,
                             jnp.where(lane == 1, i2 - N_GROUPS,
                                       jnp.where(lane == 2, r1, jnp.where(lane == 3, r2, 0))))
    gate_ref[...] = jnp.where(lane == 0, gate1, jnp.where(lane == 1, gate2, 0.0))


def _router(h, g, w_hi, w_lo, bias, tm):
    s, d = h.shape
    row = lambda i: (i, 0)
    fixed = lambda i: (0, 0)
    return pl.pallas_call(
        functools.partial(_router_kernel, tm=tm),
        grid=(s // tm,),
        in_specs=[
            pl.BlockSpec((tm, d), row),
            pl.BlockSpec((1, d), fixed),
            pl.BlockSpec((d, ROUTER_LANES), fixed),
            pl.BlockSpec((d, ROUTER_LANES), fixed),
            pl.BlockSpec((1, ROUTER_LANES), fixed),
        ],
        out_specs=[
            pl.BlockSpec((tm, ROUTER_LANES), row),
            pl.BlockSpec((tm, ROUTER_LANES), row),
            pl.BlockSpec((1, ROUTER_LANES), fixed),
        ],
        out_shape=[
            jax.ShapeDtypeStruct((s, ROUTER_LANES), jnp.int32),
            jax.ShapeDtypeStruct((s, ROUTER_LANES), F32),
            jax.ShapeDtypeStruct((1, ROUTER_LANES), F32),
        ],
        scratch_shapes=[pltpu.VMEM((1, ROUTER_LANES), F32)],
        compiler_params=_params("arbitrary"),
        name="moe_router",
    )(h, g, w_hi, w_lo, bias)


def _row_copy(src_ref, src_row, dst_ref, dst_row, sem):
    return pltpu.make_async_copy(src_ref.at[pl.ds(src_row, 1)], dst_ref.at[pl.ds(dst_row, 1)], sem)


def _pack_bf16_pair(hi, lo):
    hi_bits = lax.bitcast_convert_type(hi.astype(BF16).astype(F32), jnp.uint32)
    lo_bits = lax.bitcast_convert_type(lo.astype(BF16).astype(F32), jnp.uint32)
    return hi_bits | (lo_bits >> 16)


def _unpack_bf16_pair(packed):
    hi = lax.bitcast_convert_type(packed & jnp.uint32(0xFFFF0000), F32)
    lo = lax.bitcast_convert_type(packed << 16, F32)
    return hi, lo


def _wait_rows(src_ref, dst_ref, sem, n):
    pltpu.make_async_copy(src_ref.at[pl.ds(0, n)], dst_ref.at[pl.ds(0, n)], sem).wait()


def _dispatch_kernel(pos_ref, h_ref, g_ref, xs_ref, buf_ref, sem, *, tb, n_steps):
    i = pl.program_id(0)
    s = tb * n_steps
    slot = lax.rem(i, 2)
    half = h_ref.shape[1] // 2

    def wait_slot(sl):
        _wait_rows(buf_ref.at[sl], xs_ref, sem.at[sl], tb)
        _wait_rows(buf_ref.at[sl], xs_ref, sem.at[sl], tb)

    @pl.when(i >= 2)
    def _():
        wait_slot(slot)

    def norm_rows(r, carry):
        rows = pl.ds(pl.multiple_of(r * ROW_CHUNK, ROW_CHUNK), ROW_CHUNK)
        xn = _rms(h_ref[rows, :], g_ref[...])
        buf_ref[slot, rows, :] = _pack_bf16_pair(xn[:, :half], xn[:, half:])
        return carry

    lax.fori_loop(0, tb // ROW_CHUNK, norm_rows, 0, unroll=4)
    base = i * tb

    def issue(t, carry):
        _row_copy(buf_ref.at[slot], t, xs_ref, pos_ref[base + t], sem.at[slot]).start()
        _row_copy(buf_ref.at[slot], t, xs_ref, pos_ref[s + base + t], sem.at[slot]).start()
        return carry

    lax.fori_loop(0, tb, issue, 0, unroll=DMA_ISSUE_UNROLL)

    @pl.when(i == n_steps - 1)
    def _():
        if n_steps > 1:
            wait_slot(1 - slot)
        wait_slot(slot)


def _dispatch(pos, h, g, tb):
    s, d = h.shape
    n_rows = pos.shape[0]
    n_steps = s // tb
    return pl.pallas_call(
        functools.partial(_dispatch_kernel, tb=tb, n_steps=n_steps),
        grid_spec=pltpu.PrefetchScalarGridSpec(
            num_scalar_prefetch=1,
            grid=(n_steps,),
            in_specs=[pl.BlockSpec((tb, d), lambda i, pos: (i, 0)),
                      pl.BlockSpec((1, d), lambda i, pos: (0, 0))],
            out_specs=pl.BlockSpec(memory_space=pl.ANY),
            scratch_shapes=[pltpu.VMEM((2, tb, d // 2), jnp.uint32),
                            pltpu.SemaphoreType.DMA((2,))],
        ),
        out_shape=jax.ShapeDtypeStruct((n_rows, d // 2), jnp.uint32),
        compiler_params=_params("arbitrary"),
        name="moe_dispatch",
    )(pos, h, g)


def _expert_kernel(tile_ref, exp_ref, lo_ref, hi_ref, first_ref, total_ref, x_ref, wg_ref, wu_ref,
                   wd_ref, o_ref):
    p = pl.program_id(0)

    @pl.when(p < total_ref[0])
    def _():
        half = wg_ref.shape[1] // 2
        x_hi, x_lo = _unpack_bf16_pair(x_ref[...])
        x_hi, x_lo = x_hi.astype(BF16), x_lo.astype(BF16)
        a = (jnp.dot(x_hi, wg_ref[0, :half, :], preferred_element_type=F32)
             + jnp.dot(x_lo, wg_ref[0, half:, :], preferred_element_type=F32))
        b = (jnp.dot(x_hi, wu_ref[0, :half, :], preferred_element_type=F32)
             + jnp.dot(x_lo, wu_ref[0, half:, :], preferred_element_type=F32))
        row = lax.broadcasted_iota(jnp.int32, (x_hi.shape[0], 1), 0)
        mine = (row >= lo_ref[p]) & (row < hi_ref[p])
        hid = (a * jax.nn.sigmoid(a) * b).astype(BF16)
        y = jnp.dot(hid, wd_ref[0], preferred_element_type=F32)
        packed = _pack_bf16_pair(y[:, :half], y[:, half:])

        @pl.when(first_ref[p] == 1)
        def _():
            o_ref[...] = packed

        @pl.when(first_ref[p] == 0)
        def _():
            o_ref[...] = jnp.where(mine, packed, o_ref[...])


def _experts(tables, xs, w_gate, w_up, w_down):
    n_rows, half = xs.shape
    d = 2 * half
    tm = EXPERT_TILE
    rows = lambda p, tile, exp, *_: (tile[p], 0)
    wsel = lambda p, tile, exp, *_: (exp[p], 0, 0)
    return pl.pallas_call(
        _expert_kernel,
        grid_spec=pltpu.PrefetchScalarGridSpec(
            num_scalar_prefetch=len(tables),
            grid=(tables[0].shape[0],),
            in_specs=[
                pl.BlockSpec((tm, half), rows),
                pl.BlockSpec((1, d, EXPERT_FF), wsel),
                pl.BlockSpec((1, d, EXPERT_FF), wsel),
                pl.BlockSpec((1, EXPERT_FF, d), wsel),
            ],
            out_specs=pl.BlockSpec((tm, half), rows),
        ),
        out_shape=jax.ShapeDtypeStruct((n_rows, half), jnp.uint32),
        compiler_params=_params("arbitrary"),
        name="moe_experts",
    )(*tables, xs, w_gate, w_up, w_down)


def _combine_kernel(pos_ref, h_ref, gate_ref, g_ref, ys_ref, *refs, tb, n_steps, final):
    if final:
        o_ref, y_ref, sem = refs
    else:
        o_ref, on_ref, y_ref, sem = refs
    i = pl.program_id(0)
    s = tb * n_steps
    slot = lax.rem(i, 2)
    half = h_ref.shape[1] // 2

    def fetch(block, sl):
        base = block * tb

        def issue(t, carry):
            _row_copy(ys_ref, pos_ref[base + t], y_ref.at[sl, 0], t, sem.at[sl]).start()
            _row_copy(ys_ref, pos_ref[s + base + t], y_ref.at[sl, 1], t, sem.at[sl]).start()
            return carry

        lax.fori_loop(0, tb, issue, 0, unroll=DMA_ISSUE_UNROLL)

    @pl.when(i == 0)
    def _():
        fetch(0, 0)

    @pl.when(i + 1 < n_steps)
    def _():
        fetch(i + 1, 1 - slot)

    _wait_rows(ys_ref, y_ref.at[slot, 0], sem.at[slot], tb)
    _wait_rows(ys_ref, y_ref.at[slot, 1], sem.at[slot], tb)
    def combine_rows(r, carry):
        rows = pl.ds(pl.multiple_of(r * ROW_CHUNK, ROW_CHUNK), ROW_CHUNK)
        gates = gate_ref[rows, :]
        g1, g2 = gates[:, 0:1], gates[:, 1:2]
        hi1, lo1 = _unpack_bf16_pair(y_ref[slot, 0, rows, :])
        hi2, lo2 = _unpack_bf16_pair(y_ref[slot, 1, rows, :])
        new_hi = h_ref[rows, :half] + g1 * hi1 + g2 * hi2
        new_lo = h_ref[rows, half:] + g1 * lo1 + g2 * lo2
        sq = (jnp.sum(new_hi * new_hi, axis=-1, keepdims=True)
              + jnp.sum(new_lo * new_lo, axis=-1, keepdims=True))
        inv = lax.rsqrt(sq / (2 * half) + NORM_EPS)
        if final:
            o_ref[rows, :half] = new_hi * inv * g_ref[:, :half]
            o_ref[rows, half:] = new_lo * inv * g_ref[:, half:]
        else:
            o_ref[rows, :half] = new_hi
            o_ref[rows, half:] = new_lo
            on_ref[rows, :half] = (new_hi * inv * g_ref[:, :half]).astype(on_ref.dtype)
            on_ref[rows, half:] = (new_lo * inv * g_ref[:, half:]).astype(on_ref.dtype)
        return carry

    lax.fori_loop(0, tb // ROW_CHUNK, combine_rows, 0, unroll=4)


def _combine(pos, h, gates, g_norm, ys, tb, final):
    s, d = h.shape
    n_steps = s // tb
    row = lambda i, pos: (i, 0)
    out_specs = [pl.BlockSpec((tb, d), row)]
    out_shape = [jax.ShapeDtypeStruct((s, d), F32)]
    if not final:
        out_specs.append(pl.BlockSpec((tb, d), row))
        out_shape.append(jax.ShapeDtypeStruct((s, d), BF16))
    return pl.pallas_call(
        functools.partial(_combine_kernel, tb=tb, n_steps=n_steps, final=final),
        grid_spec=pltpu.PrefetchScalarGridSpec(
            num_scalar_prefetch=1,
            grid=(n_steps,),
            in_specs=[
                pl.BlockSpec((tb, d), row),
                pl.BlockSpec((tb, ROUTER_LANES), row),
                pl.BlockSpec((1, d), lambda i, pos: (0, 0)),
                pl.BlockSpec(memory_space=pl.ANY),
            ],
            out_specs=out_specs,
            scratch_shapes=[pltpu.VMEM((2, 2, tb, d // 2), jnp.uint32),
                            pltpu.SemaphoreType.DMA((2,))],
        ),
        out_shape=out_shape,
        compiler_params=_params("arbitrary"),
        name="moe_combine",
    )(pos, h, gates, g_norm, ys)


def _positions_kernel(idx_ref, start_ref, o_ref):
    idx = idx_ref[...]
    lane = lax.broadcasted_iota(jnp.int32, idx.shape, 1)
    start = start_ref[...]

    def column(c):
        return jnp.sum(jnp.where(lane == c, idx, 0), axis=-1, keepdims=True)

    def seg_start(e):
        return jnp.sum(jnp.where(lane == e, start, 0), axis=-1, keepdims=True)

    pos1 = seg_start(column(0)) + column(2)
    pos2 = seg_start(column(1)) + column(3)
    o_ref[...] = jnp.where(lane == 0, pos1, jnp.where(lane == 1, pos2, 0))


def _positions(idx, seg_start, tb):
    s = idx.shape[0]
    start_row = jnp.pad(seg_start, (0, ROUTER_LANES - N_EXPERTS)).reshape(1, ROUTER_LANES)
    out = pl.pallas_call(
        _positions_kernel,
        grid=(s // tb,),
        in_specs=[pl.BlockSpec((tb, ROUTER_LANES), lambda i: (i, 0)),
                  pl.BlockSpec((1, ROUTER_LANES), lambda i: (0, 0))],
        out_specs=pl.BlockSpec((tb, ROUTER_LANES), lambda i: (i, 0)),
        out_shape=jax.ShapeDtypeStruct((s, ROUTER_LANES), jnp.int32),
        compiler_params=_params("parallel"),
        name="moe_positions",
    )(idx, start_row)
    return jnp.concatenate([out[:, 0], out[:, 1]])


def _cast_kernel(x_ref, o_ref):
    o_ref[...] = x_ref[...].astype(o_ref.dtype)


def _cast_expert_weights(w, layer):
    _, e, a, b = w.shape
    return pl.pallas_call(
        _cast_kernel,
        grid=(e,),
        in_specs=[pl.BlockSpec((None, 1, a, b), lambda i: (layer, i, 0, 0))],
        out_specs=pl.BlockSpec((1, a, b), lambda i: (i, 0, 0)),
        out_shape=jax.ShapeDtypeStruct((e, a, b), BF16),
        compiler_params=_params("parallel"),
        name="cast_expert_weights",
    )(w)


def _count_le(sorted_vals, queries):
    return jnp.sum(sorted_vals[None, :] <= queries[:, None], axis=1).astype(jnp.int32)


def _moe(h, g, g_next, final, layer, w_group, b_group, w_expert, b_expert, w_gate, w_up, w_down):
    s, d = h.shape
    pad = ROUTER_LANES - N_GROUPS - N_EXPERTS
    w_r = jnp.pad(jnp.concatenate([w_group, w_expert], axis=1), ((0, 0), (0, pad)))
    w_hi = w_r.astype(BF16)
    w_lo = (w_r - w_hi.astype(F32)).astype(BF16)
    b_r = jnp.pad(jnp.concatenate([b_group, b_expert]), (0, pad)).reshape(1, ROUTER_LANES)
    idx, gates, counts = _router(h, g, w_hi, w_lo, b_r.astype(F32), tm=min(512, s))

    tm = EXPERT_TILE
    n_tiles = 2 * s // tm
    n_pairs = n_tiles + N_EXPERTS - 1
    counts = counts[0, N_GROUPS:N_GROUPS + N_EXPERTS].astype(jnp.int32)
    seg_end = jnp.cumsum(counts)
    seg_start = seg_end - counts
    pos = _positions(idx, seg_start, tb=min(2048, s))
    tile_row = jnp.arange(n_tiles, dtype=jnp.int32) * tm
    first_e = _count_le(seg_end, tile_row)
    last_e = _count_le(seg_end, tile_row + tm - 1)
    per_tile = last_e - first_e + 1
    pair_end = jnp.cumsum(per_tile)
    pair_start = pair_end - per_tile
    total = pair_end[-1]
    p = jnp.minimum(jnp.arange(n_pairs, dtype=jnp.int32), total - 1)
    p_tile = _count_le(pair_end, p)
    p_expert = first_e[p_tile] + p - pair_start[p_tile]
    p_lo = jnp.clip(seg_start[p_expert] - p_tile * tm, 0, tm)
    p_hi = jnp.clip(seg_end[p_expert] - p_tile * tm, 0, tm)
    p_first = (p_expert == first_e[p_tile]).astype(jnp.int32)

    xs = _dispatch(pos, h, g, tb=min(256, s))
    ys = _experts((p_tile, p_expert, p_lo, p_hi, p_first, total.reshape(1)), xs,
                  _cast_expert_weights(w_gate, layer), _cast_expert_weights(w_up, layer),
                  _cast_expert_weights(w_down, layer))
    return _combine(pos, h, gates, g_next, ys, tb=min(256, s), final=final)


def _rope_tables(s):
    inv_freq = ROPE_THETA ** (-jnp.arange(0, MLA_ROPE_DIM, 2, dtype=F32) / MLA_ROPE_DIM)
    ang = jnp.arange(s, dtype=F32)[:, None] * inv_freq[None, :]
    cos, sin = jnp.cos(ang), jnp.sin(ang)
    zeros = jnp.zeros((s, 128 - MLA_ROPE_DIM), F32)
    return (jnp.concatenate([cos, cos, zeros], axis=1), jnp.concatenate([sin, sin, zeros], axis=1))


def _rot_half_cols(w):
    half = MLA_ROPE_DIM // 2
    return jnp.concatenate([-w[..., half:], w[..., :half]], axis=-1)


def _even_mixer(h, hn, w_in, pool_w, pool_scale, q_norm, w_uq, kv_norm, w_ukv, w_out, ck, sk):
    s, d = h.shape
    w_kpe = w_in[:, POOL_DIM + MLA_Q_RANK + MLA_KV_RANK:]
    w_in_ext = jnp.concatenate([w_in, _rot_half_cols(w_kpe)], axis=1).astype(BF16)
    big = min(1024, s)
    z = _matmul([hn], [w_in_ext], BF16, tm=big, tn=896, name="even_in_proj")

    ya = _pool_mixer(z, pool_w.astype(BF16), pool_scale.reshape(1, POOL_DIM), ts=min(512, s))

    w_q = w_uq.reshape(MLA_Q_RANK, MLA_HEADS, MLA_QK_DIM)
    w_q = jnp.concatenate([w_q, _rot_half_cols(w_q[..., MLA_NOPE_DIM:])], axis=-1)
    w_q = jnp.transpose(w_q, (1, 0, 2)).astype(BF16)
    w_kv = w_ukv.reshape(MLA_KV_RANK, MLA_HEADS // 2, 2, MLA_NOPE_DIM + MLA_V_DIM)
    w_k = jnp.transpose(w_kv[..., :MLA_NOPE_DIM], (1, 0, 2, 3))
    w_k = w_k.reshape(MLA_HEADS // 2, MLA_KV_RANK, 2 * MLA_NOPE_DIM).astype(BF16)
    w_vt = jnp.transpose(w_kv[..., MLA_NOPE_DIM:], (1, 2, 3, 0))
    w_vt = w_vt.reshape(MLA_HEADS // 2, 2 * MLA_V_DIM, MLA_KV_RANK).astype(BF16)
    q = _mla_q(z, q_norm.reshape(1, -1), w_q, ck, sk, tm=big, hb=4)
    k, vt = _mla_kv(z, kv_norm.reshape(1, -1), w_k, w_vt, ck, sk, tm=big, hb=4)
    yb = _mla_attention(q, k, vt, tb=min(512, s), n_sub=4 if s >= 2048 else 1)

    w_o = w_out.astype(BF16)
    return _matmul([ya, yb], [w_o[:POOL_DIM], w_o[POOL_DIM:]], F32, tm=min(512, s), tn=1024,
                   residual=h, name="even_out_proj")


def _odd_mixer(h, hn, w_in, ln_g, ln_b, w_s, b_s, sinks, rel_bias, w_out):
    s, d = h.shape
    big = min(1024, s)
    z = _matmul([hn], [w_in.astype(BF16)], BF16, tm=big, tn=1024, name="odd_in_proj")
    yc = _sgu_mixer(z, ln_g.reshape(1, -1), ln_b.reshape(1, -1), w_s, jnp.transpose(b_s),
                    ts=min(256, s))
    yd = _swa_mixer(z, sinks, _swa_bias(rel_bias))
    w_o = w_out.astype(BF16)
    return _matmul([yc, yd], [w_o[:SGU_DIM], w_o[SGU_DIM:]], F32, tm=min(512, s), tn=1024,
                   residual=h, name="odd_out_proj")


def kernel(x, mem, norm_mix, norm_xattn, norm_ffn, norm_mem, final_norm, rel_bias, e_w_in, pool_w, pool_scale, mla_q_norm, mla_w_uq, mla_kv_norm, mla_w_ukv, e_w_out, o_w_in, sgu_ln_g, sgu_ln_b, sgu_w, sgu_b, swa_sinks, o_w_out, xa_wq, xa_wk, xa_wv, xa_wo, moe_w_group, moe_b_group, moe_w_expert, moe_b_expert, moe_w_gate, moe_w_up, moe_w_down):
    batch, s, d = x.shape
    assert batch == 1
    depth = norm_mix.shape[0]
    ck, sk = _rope_tables(s)
    h = x.reshape(s, d)
    mem2 = mem.reshape(mem.shape[1], d)
    hn = _rmsnorm(h, norm_mix[0], BF16, tm=min(512, s))
    for layer in range(depth):
        i = layer // 2
        last = layer == depth - 1
        if layer % 2 == 0:
            h = _even_mixer(h, hn, e_w_in[i], pool_w[i], pool_scale[i], mla_q_norm[i], mla_w_uq[i],
                            mla_kv_norm[i], mla_w_ukv[i], e_w_out[i], ck, sk)
        else:
            h = _odd_mixer(h, hn, o_w_in[i], sgu_ln_g[i], sgu_ln_b[i], sgu_w[i], sgu_b[i],
                           swa_sinks[i], rel_bias, o_w_out[i])
        mem_n = _rmsnorm(mem2, norm_mem[layer], BF16, tm=mem2.shape[0])
        w_kv_mem = jnp.concatenate([xa_wk[layer], xa_wv[layer]], axis=1).astype(BF16)
        kv_mem = _matmul([mem_n], [w_kv_mem], BF16, tm=mem2.shape[0], tn=2 * XATTN_DIM,
                         name="mem_kv_proj")
        h = _xattn(h, norm_xattn[layer].reshape(1, d), xa_wq[layer].astype(BF16), kv_mem,
                   xa_wo[layer].astype(BF16), tm=min(256, s))
        g_next = final_norm if last else norm_mix[layer + 1]
        res = _moe(h, norm_ffn[layer].reshape(1, d), g_next.reshape(1, d), last, layer,
                   moe_w_group[layer], moe_b_group[layer], moe_w_expert[layer],
                   moe_b_expert[layer], moe_w_gate, moe_w_up, moe_w_down)
        if last:
            out, = res
        else:
            h, hn = res
    return out.reshape(batch, s, d)
```

```python
import functools
import math

import jax
import jax.numpy as jnp
from jax import lax
from jax.experimental import pallas as pl
from jax.experimental.pallas import tpu as pltpu

F32 = jnp.float32
BF16 = jnp.bfloat16

NORM_EPS = 1e-6
NEG_INF = -1e30

POOL_WINDOWS = (2, 4, 8, 16)
POOL_GROUP_DIM = 256
POOL_DIM = 1024
POOL_HALO = 16

MLA_HEADS = 24
MLA_Q_RANK = 1024
MLA_KV_RANK = 512
MLA_NOPE_DIM = 128
MLA_ROPE_DIM = 64
MLA_V_DIM = 128
MLA_QK_DIM = MLA_NOPE_DIM + MLA_ROPE_DIM
ROPE_THETA = 10000.0

SGU_GROUPS = 8
SGU_GROUP_DIM = 256
SGU_DIM = 2048
SGU_CHUNK = 128

SWA_HEADS = 32
SWA_KV_HEADS = 8
SWA_HEAD_DIM = 64
SWA_BLOCK = 128
REL_BUCKETS = 32
REL_MAX_DIST = 128

XATTN_HEADS = 4
XATTN_HEAD_DIM = 128
XATTN_DIM = 512

N_GROUPS = 4
EXPERTS_PER_GROUP = 8
N_EXPERTS = 32
EXPERT_FF = 512
ROUTER_LANES = 128
EXPERT_TILE = 256
DMA_ISSUE_UNROLL = 8
ROW_CHUNK = 16
SIDE_CAST_BLOCKS = 128

VMEM_LIMIT_BYTES = 56 * 1024 * 1024


def _params(*semantics):
    return pltpu.CompilerParams(dimension_semantics=semantics, vmem_limit_bytes=VMEM_LIMIT_BYTES)


def _rms(x, g):
    return x * lax.rsqrt(jnp.mean(x * x, axis=-1, keepdims=True) + NORM_EPS) * g


def _rmsnorm_kernel(x_ref, g_ref, o_ref):
    o_ref[...] = _rms(x_ref[...].astype(F32), g_ref[...]).astype(o_ref.dtype)


def _rmsnorm(x, g, out_dtype, tm):
    m, d = x.shape
    return pl.pallas_call(
        _rmsnorm_kernel,
        grid=(m // tm,),
        in_specs=[pl.BlockSpec((tm, d), lambda i: (i, 0)), pl.BlockSpec((1, d), lambda i: (0, 0))],
        out_specs=pl.BlockSpec((tm, d), lambda i: (i, 0)),
        out_shape=jax.ShapeDtypeStruct((m, d), out_dtype),
        compiler_params=_params("parallel"),
        name="rmsnorm",
    )(x, g.reshape(1, d).astype(F32))


def _side_cast_plan(w, layer, host_steps, step_of):
    depth, e, a, b = w.shape
    n_blocks = SIDE_CAST_BLOCKS
    while n_blocks > host_steps:
        n_blocks //= 2
    rows = e * a // n_blocks

    def block_of(*idx):
        return jnp.minimum(step_of(*idx), n_blocks - 1)

    return dict(
        operand=w.reshape(depth, e * a, b),
        in_spec=pl.BlockSpec((None, rows, b), lambda *idx: (layer, block_of(*idx), 0)),
        out_spec=pl.BlockSpec((rows, b), lambda *idx: (block_of(*idx), 0)),
        out_shape=jax.ShapeDtypeStruct((e * a, b), BF16),
        shape=(e, a, b),
    )


def _host_call(body, side, operands, *, grid, in_specs, out_spec, out_shape, semantics, name,
               step_of, scratch_shapes=()):
    plans = [_side_cast_plan(w, layer, math.prod(grid), step_of) for w, layer in side]
    n_in, n_side = len(in_specs), len(plans)

    def wrapped(*refs):
        side_in = refs[n_in:n_in + n_side]
        side_out = refs[n_in + n_side + 1:n_in + 2 * n_side + 1]
        for src, dst in zip(side_in, side_out):
            dst[...] = src[...].astype(dst.dtype)
        body(*refs[:n_in], refs[n_in + n_side], *refs[n_in + 2 * n_side + 1:])

    res = pl.pallas_call(
        wrapped,
        grid=grid,
        in_specs=list(in_specs) + [p["in_spec"] for p in plans],
        out_specs=[out_spec] + [p["out_spec"] for p in plans],
        out_shape=[out_shape] + [p["out_shape"] for p in plans],
        scratch_shapes=list(scratch_shapes),
        compiler_params=_params(*semantics),
        name=name,
    )(*operands, *[p["operand"] for p in plans])
    return res[0], [r.reshape(p["shape"]) for r, p in zip(res[1:], plans)]


def _matmul_kernel(*refs, n_in, has_res):
    o_ref = refs[-1]
    acc = jnp.dot(refs[0][...], refs[n_in][...], preferred_element_type=F32)
    for k in range(1, n_in):
        acc += jnp.dot(refs[k][...], refs[n_in + k][...], preferred_element_type=F32)
    if has_res:
        acc += refs[2 * n_in][...]
    o_ref[...] = acc.astype(o_ref.dtype)


def _matmul(xs, ws, out_dtype, tm, tn, residual=None, name="matmul", side=()):
    m = xs[0].shape[0]
    n = ws[0].shape[1]
    in_specs = [pl.BlockSpec((tm, x.shape[1]), lambda i, j: (i, 0)) for x in xs]
    in_specs += [pl.BlockSpec((w.shape[0], tn), lambda i, j: (0, j)) for w in ws]
    args = list(xs) + list(ws)
    if residual is not None:
        in_specs.append(pl.BlockSpec((tm, tn), lambda i, j: (i, j)))
        args.append(residual)
    n_j = n // tn
    out, casts = _host_call(
        functools.partial(_matmul_kernel, n_in=len(xs), has_res=residual is not None),
        side, args,
        grid=(m // tm, n_j),
        in_specs=in_specs,
        out_spec=pl.BlockSpec((tm, tn), lambda i, j: (i, j)),
        out_shape=jax.ShapeDtypeStruct((m, n), out_dtype),
        semantics=("arbitrary", "arbitrary") if side else ("parallel", "parallel"),
        name=name,
        step_of=lambda i, j: i * n_j + j,
    )
    return (out, casts) if side else out


def _pool_kernel(a_ref, halo_ref, w_ref, scale_ref, o_ref, *, ts):
    i = pl.program_id(0)
    g = pl.program_id(1)
    win = jnp.left_shift(2, g)
    a = a_ref[...]
    halo = jnp.where(i > 0, halo_ref[...], jnp.zeros_like(halo_ref))
    ext = jnp.concatenate([halo, a], axis=0)
    row = lax.broadcasted_iota(jnp.int32, (ts, POOL_HALO + ts), 0) + POOL_HALO
    col = lax.broadcasted_iota(jnp.int32, (ts, POOL_HALO + ts), 1)
    band = ((col <= row) & (col > row - win)).astype(BF16)
    wsum = jnp.dot(band, ext, preferred_element_type=F32)
    t = i * ts + lax.broadcasted_iota(jnp.int32, (ts, 1), 0)
    count = jnp.minimum(t + 1, win).astype(F32)
    d = (wsum / count - a.astype(F32)).astype(BF16)
    y = jnp.dot(d, w_ref[0], preferred_element_type=F32) * scale_ref[...]
    o_ref[...] = y.astype(o_ref.dtype)


def _pool_mixer(z, pool_w, pool_scale, ts):
    s = z.shape[0]
    c = POOL_GROUP_DIM
    hb = ts // POOL_HALO
    return pl.pallas_call(
        functools.partial(_pool_kernel, ts=ts),
        grid=(s // ts, len(POOL_WINDOWS)),
        in_specs=[
            pl.BlockSpec((ts, c), lambda i, g: (i, g)),
            pl.BlockSpec((POOL_HALO, c), lambda i, g: (jnp.maximum(i * hb - 1, 0), g)),
            pl.BlockSpec((1, c, c), lambda i, g: (g, 0, 0)),
            pl.BlockSpec((1, c), lambda i, g: (0, g)),
        ],
        out_specs=pl.BlockSpec((ts, c), lambda i, g: (i, g)),
        out_shape=jax.ShapeDtypeStruct((s, POOL_DIM), BF16),
        compiler_params=_params("parallel", "parallel"),
        name="pool_mixer",
    )(z, z, pool_w, pool_scale)


def _rope_half_block(blk, ck, sk):
    return blk * ck + pltpu.roll(blk, 64, 1) * sk


def _mla_q_kernel(cq_ref, g_ref, w_ref, ck_ref, sk_ref, o_ref, xn_ref, *, hb, scale):
    @pl.when(pl.program_id(1) == 0)
    def _():
        xn_ref[...] = _rms(cq_ref[...].astype(F32), g_ref[...]).astype(BF16)

    xn = xn_ref[...]
    for hh in range(hb):
        q = jnp.dot(xn, w_ref[hh], preferred_element_type=F32)
        o_ref[hh, :, 0:MLA_NOPE_DIM] = (q[:, :MLA_NOPE_DIM] * scale).astype(o_ref.dtype)
        pe = _rope_half_block(q[:, MLA_NOPE_DIM:], ck_ref[...], sk_ref[...]) * scale
        o_ref[hh, :, MLA_NOPE_DIM:MLA_QK_DIM] = pe[:, :MLA_ROPE_DIM].astype(o_ref.dtype)


def _mla_q(z, q_norm, w_q, ck, sk, tm, hb):
    s = z.shape[0]
    scale = MLA_QK_DIM ** -0.5 * math.log2(math.e)
    return pl.pallas_call(
        functools.partial(_mla_q_kernel, hb=hb, scale=scale),
        grid=(s // tm, MLA_HEADS // hb),
        in_specs=[
            pl.BlockSpec((tm, MLA_Q_RANK), lambda i, j: (i, POOL_DIM // MLA_Q_RANK)),
            pl.BlockSpec((1, MLA_Q_RANK), lambda i, j: (0, 0)),
            pl.BlockSpec((hb, MLA_Q_RANK, 256), lambda i, j: (j, 0, 0)),
            pl.BlockSpec((tm, 128), lambda i, j: (i, 0)),
            pl.BlockSpec((tm, 128), lambda i, j: (i, 0)),
        ],
        out_specs=pl.BlockSpec((hb, tm, MLA_QK_DIM), lambda i, j: (j, i, 0)),
        out_shape=jax.ShapeDtypeStruct((MLA_HEADS, s, MLA_QK_DIM), BF16),
        scratch_shapes=[pltpu.VMEM((tm, MLA_Q_RANK), BF16)],
        compiler_params=_params("parallel", "arbitrary"),
        name="mla_q_proj",
    )(z, q_norm, w_q, ck, sk)


def _mla_kv_kernel(ckv_ref, kpe_ref, g_ref, wk_ref, wvt_ref, ck_ref, sk_ref, k_ref, vt_ref,
                   xn_ref, kr_ref, *, hb):
    @pl.when(pl.program_id(1) == 0)
    def _():
        xn_ref[...] = _rms(ckv_ref[...].astype(F32), g_ref[...]).astype(BF16)
        kr = _rope_half_block(kpe_ref[...].astype(F32), ck_ref[...], sk_ref[...])
        kr_ref[...] = kr.astype(BF16)

    xn = xn_ref[...]
    for pair in range(hb // 2):
        kk = jnp.dot(xn, wk_ref[pair], preferred_element_type=F32)
        vv = lax.dot_general(wvt_ref[pair], xn, (((1,), (1,)), ((), ())),
                             preferred_element_type=F32)
        for side in range(2):
            hh = 2 * pair + side
            k_ref[hh, :, 0:MLA_NOPE_DIM] = (
                kk[:, side * MLA_NOPE_DIM:(side + 1) * MLA_NOPE_DIM].astype(k_ref.dtype))
            k_ref[hh, :, MLA_NOPE_DIM:MLA_QK_DIM] = kr_ref[:, 0:MLA_ROPE_DIM]
            vt_ref[hh] = vv[side * MLA_V_DIM:(side + 1) * MLA_V_DIM, :].astype(vt_ref.dtype)


def _mla_kv(z, kv_norm, w_k, w_vt, ck, sk, tm, hb):
    s = z.shape[0]
    ckv_block = (POOL_DIM + MLA_Q_RANK) // MLA_KV_RANK
    kpe_block = (POOL_DIM + MLA_Q_RANK + MLA_KV_RANK) // 128
    return pl.pallas_call(
        functools.partial(_mla_kv_kernel, hb=hb),
        grid=(s // tm, MLA_HEADS // hb),
        in_specs=[
            pl.BlockSpec((tm, MLA_KV_RANK), lambda i, j: (i, ckv_block)),
            pl.BlockSpec((tm, 128), lambda i, j: (i, kpe_block)),
            pl.BlockSpec((1, MLA_KV_RANK), lambda i, j: (0, 0)),
            pl.BlockSpec((hb // 2, MLA_KV_RANK, 2 * MLA_NOPE_DIM), lambda i, j: (j, 0, 0)),
            pl.BlockSpec((hb // 2, 2 * MLA_V_DIM, MLA_KV_RANK), lambda i, j: (j, 0, 0)),
            pl.BlockSpec((tm, 128), lambda i, j: (i, 0)),
            pl.BlockSpec((tm, 128), lambda i, j: (i, 0)),
        ],
        out_specs=[
            pl.BlockSpec((hb, tm, MLA_QK_DIM), lambda i, j: (j, i, 0)),
            pl.BlockSpec((hb, MLA_V_DIM, tm), lambda i, j: (j, 0, i)),
        ],
        out_shape=[
            jax.ShapeDtypeStruct((MLA_HEADS, s, MLA_QK_DIM), BF16),
            jax.ShapeDtypeStruct((MLA_HEADS, MLA_V_DIM, s), BF16),
        ],
        scratch_shapes=[pltpu.VMEM((tm, MLA_KV_RANK), BF16), pltpu.VMEM((tm, 128), BF16)],
        compiler_params=_params("parallel", "arbitrary"),
        name="mla_kv_proj",
    )(z, z, kv_norm, w_k, w_vt, ck, sk)


def _mla_attn_kernel(q_ref, k_ref, v_ref, o_ref, m_ref, l_ref, acc_ref, s_ref, *, tb, n_sub):
    i = pl.program_id(1)
    m_ref[...] = jnp.full_like(m_ref, NEG_INF)
    l_ref[...] = jnp.zeros_like(l_ref)
    acc_ref[...] = jnp.zeros_like(acc_ref)

    def scores(x, kv):
        start = pl.multiple_of(kv * tb, tb)
        q = q_ref[0, x * tb:(x + 1) * tb, :]
        k = k_ref[0, pl.ds(start, tb), :]
        return lax.dot_general(k, q, (((1,), (1,)), ((), ())), preferred_element_type=F32)

    def update(x, kv, s, masked):
        start = pl.multiple_of(kv * tb, tb)
        vt = v_ref[0, :, pl.ds(start, tb)]
        if masked:
            key = lax.broadcasted_iota(jnp.int32, (tb, tb), 0)
            qry = lax.broadcasted_iota(jnp.int32, (tb, tb), 1)
            s = jnp.where(key <= qry, s, NEG_INF)
        m_prev = m_ref[x]
        m_new = jnp.maximum(m_prev, jnp.max(s, axis=0, keepdims=True))
        alpha = jnp.exp2(m_prev - m_new)
        p = jnp.exp2(s - m_new)
        l_ref[x] = alpha * l_ref[x] + jnp.sum(p, axis=0, keepdims=True)
        acc_ref[x] = alpha * acc_ref[x] + jnp.dot(vt, p.astype(BF16), preferred_element_type=F32)
        m_ref[x] = m_new

    for x in range(n_sub):
        s_ref[x] = scores(x, 0)

    def body(kv, carry):
        for x in range(n_sub):
            s = s_ref[x]
            s_ref[x] = scores(x, kv + 1)
            update(x, kv, s, False)
        return carry

    first_diag = n_sub * i
    lax.fori_loop(0, first_diag, body, 0)
    for d in range(n_sub):
        for x in range(d, n_sub):
            s = s_ref[x] if d == 0 else scores(x, first_diag + d)
            update(x, first_diag + d, s, x == d)
    for x in range(n_sub):
        o_ref[x * tb:(x + 1) * tb, :] = jnp.transpose(acc_ref[x] / l_ref[x]).astype(o_ref.dtype)


def _mla_attention(q, k, v, tb, n_sub, side=()):
    s = q.shape[1]
    tq = tb * n_sub
    n_q = s // tq
    return _host_call(
        functools.partial(_mla_attn_kernel, tb=tb, n_sub=n_sub),
        side, (q, k, v),
        grid=(MLA_HEADS, n_q),
        in_specs=[
            pl.BlockSpec((1, tq, MLA_QK_DIM), lambda h, i: (h, i, 0)),
            pl.BlockSpec((1, s, MLA_QK_DIM), lambda h, i: (h, 0, 0)),
            pl.BlockSpec((1, MLA_V_DIM, s), lambda h, i: (h, 0, 0)),
        ],
        out_spec=pl.BlockSpec((tq, MLA_V_DIM), lambda h, i: (i, h)),
        out_shape=jax.ShapeDtypeStruct((s, MLA_HEADS * MLA_V_DIM), BF16),
        scratch_shapes=[pltpu.VMEM((n_sub, 1, tb), F32), pltpu.VMEM((n_sub, 1, tb), F32),
                        pltpu.VMEM((n_sub, MLA_V_DIM, tb), F32),
                        pltpu.VMEM((n_sub, tb, tb), F32)],
        semantics=("arbitrary", "arbitrary"),
        name="mla_attention",
        step_of=lambda h, i: h * n_q + i,
    )


def _gelu_tanh(x):
    return 0.5 * x * (1.0 + jnp.tanh(math.sqrt(2.0 / math.pi) * (x + 0.044715 * (x * x * x))))


def _sgu_kernel(u_ref, v_ref, g_ref, b_ref, w_ref, bs_ref, o_ref, *, ts):
    row = lax.broadcasted_iota(jnp.int32, (SGU_CHUNK, SGU_CHUNK), 0)
    col = lax.broadcasted_iota(jnp.int32, (SGU_CHUNK, SGU_CHUNK), 1)
    causal = col <= row
    for c in range(ts // SGU_CHUNK):
        rows = slice(c * SGU_CHUNK, (c + 1) * SGU_CHUNK)
        v = _gelu_tanh(v_ref[rows, :].astype(F32))
        mu = jnp.mean(v, axis=-1, keepdims=True)
        vc = v - mu
        vn = vc * lax.rsqrt(jnp.mean(vc * vc, axis=-1, keepdims=True) + NORM_EPS)
        vn = (vn * g_ref[...] + b_ref[...]).astype(BF16)
        for g in range(SGU_GROUPS):
            cols = slice(g * SGU_GROUP_DIM, (g + 1) * SGU_GROUP_DIM)
            w = jnp.where(causal, w_ref[g], 0.0).astype(BF16)
            mixed = jnp.dot(w, vn[:, cols], preferred_element_type=F32) + bs_ref[:, g:g + 1]
            u = _gelu_tanh(u_ref[rows, cols].astype(F32))
            o_ref[rows, cols] = (u * mixed).astype(o_ref.dtype)


def _sgu_mixer(z, ln_g, ln_b, w_s, b_s_t, ts):
    s = z.shape[0]
    return pl.pallas_call(
        functools.partial(_sgu_kernel, ts=ts),
        grid=(s // ts,),
        in_specs=[
            pl.BlockSpec((ts, SGU_DIM), lambda i: (i, 0)),
            pl.BlockSpec((ts, SGU_DIM), lambda i: (i, 1)),
            pl.BlockSpec((1, SGU_DIM), lambda i: (0, 0)),
            pl.BlockSpec((1, SGU_DIM), lambda i: (0, 0)),
            pl.BlockSpec((SGU_GROUPS, SGU_CHUNK, SGU_CHUNK), lambda i: (0, 0, 0)),
            pl.BlockSpec((SGU_CHUNK, SGU_GROUPS), lambda i: (0, 0)),
        ],
        out_specs=pl.BlockSpec((ts, SGU_DIM), lambda i: (i, 0)),
        out_shape=jax.ShapeDtypeStruct((s, SGU_DIM), BF16),
        compiler_params=_params("parallel"),
        name="sgu_mixer",
    )(z, z, ln_g, ln_b, w_s, b_s_t)


def _swa_kernel(sink_ref, q_ref, kp_ref, kc_ref, vp_ref, vc_ref, bias_ref, o_ref):
    n = pl.program_id(0)
    blk = SWA_BLOCK
    hd = SWA_HEAD_DIM
    col = lax.broadcasted_iota(jnp.int32, (blk, 2 * blk), 1)
    no_prev = (n == 0) & (col < blk)
    q_low = lax.broadcasted_iota(jnp.int32, (blk, 2 * hd), 1) < hd
    kv_low = lax.broadcasted_iota(jnp.int32, (2 * blk, 2 * hd), 1) < hd
    scale = hd ** -0.5
    for pair in range(SWA_KV_HEADS // 2):
        cols = slice(pair * 2 * hd, (pair + 1) * 2 * hd)
        kp = jnp.concatenate([kp_ref[:, cols], kc_ref[:, cols]], axis=0).astype(F32)
        vp = jnp.concatenate([vp_ref[:, cols], vc_ref[:, cols]], axis=0).astype(F32)
        k_swapped = pltpu.roll(kp, hd, 1)
        v_swapped = pltpu.roll(vp, hd, 1)
        for half in range(2):
            kvh = 2 * pair + half
            own = kv_low if half == 0 else ~kv_low
            k_dup = jnp.where(own, kp, k_swapped).astype(BF16)
            v_dup = jnp.where(own, vp, v_swapped)
            v_side = (jnp.where(kv_low, v_dup, 0.0).astype(BF16),
                      jnp.where(kv_low, 0.0, v_dup).astype(BF16))
            for t in range(2):
                tile = 2 * kvh + t
                qt = q_ref[:, tile * 2 * hd:(tile + 1) * 2 * hd].astype(F32) * scale
                o = None
                for side in range(2):
                    h = 2 * tile + side
                    qh = jnp.where(q_low if side == 0 else ~q_low, qt, 0.0).astype(BF16)
                    s = lax.dot_general(qh, k_dup, (((1,), (1,)), ((), ())),
                                        preferred_element_type=F32)
                    s = jnp.where(no_prev, NEG_INF, s + bias_ref[h])
                    sink = sink_ref[h]
                    m = jnp.maximum(jnp.max(s, axis=-1, keepdims=True), sink)
                    p = jnp.exp(s - m)
                    denom = jnp.sum(p, axis=-1, keepdims=True) + jnp.exp(sink - m)
                    part = jnp.dot(p.astype(BF16), v_side[side],
                                   preferred_element_type=F32) / denom
                    o = part if o is None else o + part
                o_ref[:, tile * 2 * hd:(tile + 1) * 2 * hd] = o.astype(o_ref.dtype)


def _swa_mixer(z, sinks, bias, side=()):
    s = z.shape[0]
    blk = SWA_BLOCK
    qw = SWA_HEADS * SWA_HEAD_DIM
    kw = SWA_KV_HEADS * SWA_HEAD_DIM
    q_block = 2 * SGU_DIM // qw
    k_block = (2 * SGU_DIM + qw) // kw
    v_block = k_block + 1
    prev = lambda n: jnp.maximum(n - 1, 0)
    return _host_call(
        _swa_kernel,
        side, (sinks, z, z, z, z, z, bias),
        grid=(s // blk,),
        in_specs=[
            pl.BlockSpec(memory_space=pltpu.SMEM),
            pl.BlockSpec((blk, qw), lambda n: (n, q_block)),
            pl.BlockSpec((blk, kw), lambda n: (prev(n), k_block)),
            pl.BlockSpec((blk, kw), lambda n: (n, k_block)),
            pl.BlockSpec((blk, kw), lambda n: (prev(n), v_block)),
            pl.BlockSpec((blk, kw), lambda n: (n, v_block)),
            pl.BlockSpec((SWA_HEADS, blk, 2 * blk), lambda n: (0, 0, 0)),
        ],
        out_spec=pl.BlockSpec((blk, qw), lambda n: (n, 0)),
        out_shape=jax.ShapeDtypeStruct((s, qw), BF16),
        semantics=("arbitrary",),
        name="swa_mixer",
        step_of=lambda n: n,
    )


def _t5_bucket(dist):
    max_exact = REL_BUCKETS // 2
    large = max_exact + (jnp.log(jnp.maximum(dist, 1).astype(F32) / max_exact)
                         / math.log(REL_MAX_DIST / max_exact) * (REL_BUCKETS - max_exact)).astype(jnp.int32)
    large = jnp.minimum(large, REL_BUCKETS - 1)
    return jnp.where(dist < max_exact, dist, large)


def _swa_bias(rel_bias):
    blk = SWA_BLOCK
    by_dist = rel_bias[_t5_bucket(jnp.arange(blk))].astype(F32)
    outside = jnp.full_like(by_dist, NEG_INF)
    diag = jnp.concatenate([outside, by_dist[::-1], outside], axis=0)
    period = 3 * blk
    diag = jnp.roll(diag, -(blk - 1), axis=0)
    skew = jnp.tile(diag, (blk, 1))[:blk * (period - 1)].reshape(blk, period - 1, SWA_HEADS)
    return jnp.transpose(skew[:, :2 * blk], (2, 0, 1))


def _xattn_kernel(h_ref, g_ref, wq_ref, kv_ref, wo_ref, o_ref):
    x = h_ref[...]
    xn = _rms(x, g_ref[...]).astype(BF16)
    q = jnp.dot(xn, wq_ref[...], preferred_element_type=F32) * (XATTN_HEAD_DIM ** -0.5)
    q = q.astype(BF16)
    outs = []
    for hd in range(XATTN_HEADS):
        cols = slice(hd * XATTN_HEAD_DIM, (hd + 1) * XATTN_HEAD_DIM)
        k = kv_ref[:, cols]
        v = kv_ref[:, XATTN_DIM + hd * XATTN_HEAD_DIM:XATTN_DIM + (hd + 1) * XATTN_HEAD_DIM]
        s = lax.dot_general(q[:, cols], k, (((1,), (1,)), ((), ())), preferred_element_type=F32)
        p = jnp.exp(s - jnp.max(s, axis=-1, keepdims=True))
        denom = jnp.sum(p, axis=-1, keepdims=True)
        o = jnp.dot(p.astype(BF16), v, preferred_element_type=F32) / denom
        outs.append(o.astype(BF16))
    o = jnp.concatenate(outs, axis=-1)
    o_ref[...] = x + jnp.dot(o, wo_ref[...], preferred_element_type=F32)


def _xattn(h, g, wq, kv_mem, wo, tm, side=()):
    s, d = h.shape
    mlen = kv_mem.shape[0]
    return _host_call(
        _xattn_kernel,
        side, (h, g, wq, kv_mem, wo),
        grid=(s // tm,),
        in_specs=[
            pl.BlockSpec((tm, d), lambda i: (i, 0)),
            pl.BlockSpec((1, d), lambda i: (0, 0)),
            pl.BlockSpec((d, XATTN_DIM), lambda i: (0, 0)),
            pl.BlockSpec((mlen, 2 * XATTN_DIM), lambda i: (0, 0)),
            pl.BlockSpec((XATTN_DIM, d), lambda i: (0, 0)),
        ],
        out_spec=pl.BlockSpec((tm, d), lambda i: (i, 0)),
        out_shape=jax.ShapeDtypeStruct((s, d), F32),
        semantics=("arbitrary",),
        name="memory_xattn",
        step_of=lambda i: i,
    )


def _router_kernel(h_ref, g_ref, whi_ref, wlo_ref, b_ref, idx_ref, gate_ref, cnt_ref, carry_ref,
                   *, tm):
    @pl.when(pl.program_id(0) == 0)
    def _():
        carry_ref[...] = jnp.zeros_like(carry_ref)

    xn = _rms(h_ref[...], g_ref[...])
    x_hi = xn.astype(BF16)
    x_lo = (xn - x_hi.astype(F32)).astype(BF16)
    logits = (jnp.dot(x_hi, whi_ref[...], preferred_element_type=F32)
              + jnp.dot(x_lo, whi_ref[...], preferred_element_type=F32)
              + jnp.dot(x_hi, wlo_ref[...], preferred_element_type=F32)) + b_ref[...]

    lane = lax.broadcasted_iota(jnp.int32, (tm, ROUTER_LANES), 1)
    big = jnp.int32(ROUTER_LANES)

    def first_argmax(vals):
        top = jnp.max(vals, axis=-1, keepdims=True)
        idx = jnp.min(jnp.where(vals == top, lane, big), axis=-1, keepdims=True)
        return top, idx

    is_group = lane < N_GROUPS
    g_logits = jnp.where(is_group, logits, NEG_INF)
    g_top, g_idx = first_argmax(g_logits)
    p_group = 1.0 / jnp.sum(jnp.where(is_group, jnp.exp(g_logits - g_top), 0.0), axis=-1,
                            keepdims=True)
    lo = N_GROUPS + g_idx * EXPERTS_PER_GROUP
    in_group = (lane >= lo) & (lane < lo + EXPERTS_PER_GROUP)
    e_logits = jnp.where(in_group, logits, NEG_INF)
    top1, i1 = first_argmax(e_logits)
    top2, i2 = first_argmax(jnp.where(lane == i1, NEG_INF, e_logits))
    e21 = jnp.exp(top2 - top1)
    gate1 = p_group / (1.0 + e21)
    gate2 = p_group * e21 / (1.0 + e21)

    chosen = ((lane == i1) | (lane == i2)).astype(BF16)
    r = lax.broadcasted_iota(jnp.int32, (tm, tm), 0)
    c = lax.broadcasted_iota(jnp.int32, (tm, tm), 1)
    before = (c < r).astype(BF16)
    rank = jnp.dot(before, chosen, preferred_element_type=F32) + carry_ref[...]
    carry_ref[...] += jnp.sum(chosen.astype(F32), axis=0, keepdims=True)
    cnt_ref[...] = carry_ref[...]
    r1 = jnp.sum(jnp.where(lane == i1, rank, 0.0), axis=-1, keepdims=True).astype(jnp.int32)
    r2 = jnp.sum(jnp.where(lane == i2, rank, 0.0), axis=-1, keepdims=True).astype(jnp.int32)

    idx_ref[...] = jnp.where(lane == 0, i1 - N_GROUPS,
                             jnp.where(lane == 1, i2 - N_GROUPS,
                                       jnp.where(lane == 2, r1, jnp.where(lane == 3, r2, 0))))
    gate_ref[...] = jnp.where(lane == 0, gate1, jnp.where(lane == 1, gate2, 0.0))


def _router(h, g, w_hi, w_lo, bias, tm):
    s, d = h.shape
    row = lambda i: (i, 0)
    fixed = lambda i: (0, 0)
    return pl.pallas_call(
        functools.partial(_router_kernel, tm=tm),
        grid=(s // tm,),
        in_specs=[
            pl.BlockSpec((tm, d), row),
            pl.BlockSpec((1, d), fixed),
            pl.BlockSpec((d, ROUTER_LANES), fixed),
            pl.BlockSpec((d, ROUTER_LANES), fixed),
            pl.BlockSpec((1, ROUTER_LANES), fixed),
        ],
        out_specs=[
            pl.BlockSpec((tm, ROUTER_LANES), row),
            pl.BlockSpec((tm, ROUTER_LANES), row),
            pl.BlockSpec((1, ROUTER_LANES), fixed),
        ],
        out_shape=[
            jax.ShapeDtypeStruct((s, ROUTER_LANES), jnp.int32),
            jax.ShapeDtypeStruct((s, ROUTER_LANES), F32),
            jax.ShapeDtypeStruct((1, ROUTER_LANES), F32),
        ],
        scratch_shapes=[pltpu.VMEM((1, ROUTER_LANES), F32)],
        compiler_params=_params("arbitrary"),
        name="moe_router",
    )(h, g, w_hi, w_lo, bias)


def _row_copy(src_ref, src_row, dst_ref, dst_row, sem):
    return pltpu.make_async_copy(src_ref.at[pl.ds(src_row, 1)], dst_ref.at[pl.ds(dst_row, 1)], sem)


def _pack_bf16_pair(hi, lo):
    hi_bits = lax.bitcast_convert_type(hi.astype(BF16).astype(F32), jnp.uint32)
    lo_bits = lax.bitcast_convert_type(lo.astype(BF16).astype(F32), jnp.uint32)
    return hi_bits | (lo_bits >> 16)


def _unpack_bf16_pair(packed):
    hi = lax.bitcast_convert_type(packed & jnp.uint32(0xFFFF0000), F32)
    lo = lax.bitcast_convert_type(packed << 16, F32)
    return hi, lo


def _wait_rows(src_ref, dst_ref, sem, n):
    pltpu.make_async_copy(src_ref.at[pl.ds(0, n)], dst_ref.at[pl.ds(0, n)], sem).wait()


def _dispatch_kernel(pos_ref, h_ref, g_ref, xs_ref, buf_ref, sem, *, tb, n_steps):
    i = pl.program_id(0)
    s = tb * n_steps
    slot = lax.rem(i, 2)
    half = h_ref.shape[1] // 2

    def wait_slot(sl):
        _wait_rows(buf_ref.at[sl], xs_ref, sem.at[sl], tb)
        _wait_rows(buf_ref.at[sl], xs_ref, sem.at[sl], tb)

    @pl.when(i >= 2)
    def _():
        wait_slot(slot)

    def norm_rows(r, carry):
        rows = pl.ds(pl.multiple_of(r * ROW_CHUNK, ROW_CHUNK), ROW_CHUNK)
        xn = _rms(h_ref[rows, :], g_ref[...])
        buf_ref[slot, rows, :] = _pack_bf16_pair(xn[:, :half], xn[:, half:])
        return carry

    lax.fori_loop(0, tb // ROW_CHUNK, norm_rows, 0, unroll=4)
    base = i * tb

    def issue(t, carry):
        _row_copy(buf_ref.at[slot], t, xs_ref, pos_ref[base + t], sem.at[slot]).start()
        _row_copy(buf_ref.at[slot], t, xs_ref, pos_ref[s + base + t], sem.at[slot]).start()
        return carry

    lax.fori_loop(0, tb, issue, 0, unroll=DMA_ISSUE_UNROLL)

    @pl.when(i == n_steps - 1)
    def _():
        if n_steps > 1:
            wait_slot(1 - slot)
        wait_slot(slot)


def _dispatch(pos, h, g, tb):
    s, d = h.shape
    n_rows = pos.shape[0]
    n_steps = s // tb
    return pl.pallas_call(
        functools.partial(_dispatch_kernel, tb=tb, n_steps=n_steps),
        grid_spec=pltpu.PrefetchScalarGridSpec(
            num_scalar_prefetch=1,
            grid=(n_steps,),
            in_specs=[pl.BlockSpec((tb, d), lambda i, pos: (i, 0)),
                      pl.BlockSpec((1, d), lambda i, pos: (0, 0))],
            out_specs=pl.BlockSpec(memory_space=pl.ANY),
            scratch_shapes=[pltpu.VMEM((2, tb, d // 2), jnp.uint32),
                            pltpu.SemaphoreType.DMA((2,))],
        ),
        out_shape=jax.ShapeDtypeStruct((n_rows, d // 2), jnp.uint32),
        compiler_params=_params("arbitrary"),
        name="moe_dispatch",
    )(pos, h, g)


def _expert_kernel(tile_ref, exp_ref, lo_ref, hi_ref, first_ref, total_ref, x_ref, wg_ref, wu_ref,
                   wd_ref, o_ref):
    p = pl.program_id(0)

    @pl.when(p < total_ref[0])
    def _():
        half = wg_ref.shape[1] // 2
        x_hi, x_lo = _unpack_bf16_pair(x_ref[...])
        x_hi, x_lo = x_hi.astype(BF16), x_lo.astype(BF16)
        a = (jnp.dot(x_hi, wg_ref[0, :half, :], preferred_element_type=F32)
             + jnp.dot(x_lo, wg_ref[0, half:, :], preferred_element_type=F32))
        b = (jnp.dot(x_hi, wu_ref[0, :half, :], preferred_element_type=F32)
             + jnp.dot(x_lo, wu_ref[0, half:, :], preferred_element_type=F32))
        row = lax.broadcasted_iota(jnp.int32, (x_hi.shape[0], 1), 0)
        mine = (row >= lo_ref[p]) & (row < hi_ref[p])
        hid = (a * jax.nn.sigmoid(a) * b).astype(BF16)
        y = jnp.dot(hid, wd_ref[0], preferred_element_type=F32)
        packed = _pack_bf16_pair(y[:, :half], y[:, half:])

        @pl.when(first_ref[p] == 1)
        def _():
            o_ref[...] = packed

        @pl.when(first_ref[p] == 0)
        def _():
            o_ref[...] = jnp.where(mine, packed, o_ref[...])


def _experts(tables, xs, w_gate, w_up, w_down):
    n_rows, half = xs.shape
    d = 2 * half
    tm = EXPERT_TILE
    rows = lambda p, tile, exp, *_: (tile[p], 0)
    wsel = lambda p, tile, exp, *_: (exp[p], 0, 0)
    return pl.pallas_call(
        _expert_kernel,
        grid_spec=pltpu.PrefetchScalarGridSpec(
            num_scalar_prefetch=len(tables),
            grid=(tables[0].shape[0],),
            in_specs=[
                pl.BlockSpec((tm, half), rows),
                pl.BlockSpec((1, d, EXPERT_FF), wsel),
                pl.BlockSpec((1, d, EXPERT_FF), wsel),
                pl.BlockSpec((1, EXPERT_FF, d), wsel),
            ],
            out_specs=pl.BlockSpec((tm, half), rows),
        ),
        out_shape=jax.ShapeDtypeStruct((n_rows, half), jnp.uint32),
        compiler_params=_params("arbitrary"),
        name="moe_experts",
    )(*tables, xs, w_gate, w_up, w_down)


def _combine_kernel(pos_ref, h_ref, gate_ref, g_ref, ys_ref, *refs, tb, n_steps, final):
    if final:
        o_ref, y_ref, sem = refs
    else:
        o_ref, on_ref, y_ref, sem = refs
    i = pl.program_id(0)
    s = tb * n_steps
    slot = lax.rem(i, 2)
    half = h_ref.shape[1] // 2

    def fetch(block, sl):
        base = block * tb

        def issue(t, carry):
            _row_copy(ys_ref, pos_ref[base + t], y_ref.at[sl, 0], t, sem.at[sl]).start()
            _row_copy(ys_ref, pos_ref[s + base + t], y_ref.at[sl, 1], t, sem.at[sl]).start()
            return carry

        lax.fori_loop(0, tb, issue, 0, unroll=DMA_ISSUE_UNROLL)

    @pl.when(i == 0)
    def _():
        fetch(0, 0)

    @pl.when(i + 1 < n_steps)
    def _():
        fetch(i + 1, 1 - slot)

    _wait_rows(ys_ref, y_ref.at[slot, 0], sem.at[slot], tb)
    _wait_rows(ys_ref, y_ref.at[slot, 1], sem.at[slot], tb)
    def combine_rows(r, carry):
        rows = pl.ds(pl.multiple_of(r * ROW_CHUNK, ROW_CHUNK), ROW_CHUNK)
        gates = gate_ref[rows, :]
        g1, g2 = gates[:, 0:1], gates[:, 1:2]
        hi1, lo1 = _unpack_bf16_pair(y_ref[slot, 0, rows, :])
        hi2, lo2 = _unpack_bf16_pair(y_ref[slot, 1, rows, :])
        new_hi = h_ref[rows, :half] + g1 * hi1 + g2 * hi2
        new_lo = h_ref[rows, half:] + g1 * lo1 + g2 * lo2
        sq = (jnp.sum(new_hi * new_hi, axis=-1, keepdims=True)
              + jnp.sum(new_lo * new_lo, axis=-1, keepdims=True))
        inv = lax.rsqrt(sq / (2 * half) + NORM_EPS)
        if final:
            o_ref[rows, :half] = new_hi * inv * g_ref[:, :half]
            o_ref[rows, half:] = new_lo * inv * g_ref[:, half:]
        else:
            o_ref[rows, :half] = new_hi
            o_ref[rows, half:] = new_lo
            on_ref[rows, :half] = (new_hi * inv * g_ref[:, :half]).astype(on_ref.dtype)
            on_ref[rows, half:] = (new_lo * inv * g_ref[:, half:]).astype(on_ref.dtype)
        return carry

    lax.fori_loop(0, tb // ROW_CHUNK, combine_rows, 0, unroll=4)


def _combine(pos, h, gates, g_norm, ys, tb, final):
    s, d = h.shape
    n_steps = s // tb
    row = lambda i, pos: (i, 0)
    out_specs = [pl.BlockSpec((tb, d), row)]
    out_shape = [jax.ShapeDtypeStruct((s, d), F32)]
    if not final:
        out_specs.append(pl.BlockSpec((tb, d), row))
        out_shape.append(jax.ShapeDtypeStruct((s, d), BF16))
    return pl.pallas_call(
        functools.partial(_combine_kernel, tb=tb, n_steps=n_steps, final=final),
        grid_spec=pltpu.PrefetchScalarGridSpec(
            num_scalar_prefetch=1,
            grid=(n_steps,),
            in_specs=[
                pl.BlockSpec((tb, d), row),
                pl.BlockSpec((tb, ROUTER_LANES), row),
                pl.BlockSpec((1, d), lambda i, pos: (0, 0)),
                pl.BlockSpec(memory_space=pl.ANY),
            ],
            out_specs=out_specs,
            scratch_shapes=[pltpu.VMEM((2, 2, tb, d // 2), jnp.uint32),
                            pltpu.SemaphoreType.DMA((2,))],
        ),
        out_shape=out_shape,
        compiler_params=_params("arbitrary"),
        name="moe_combine",
    )(pos, h, gates, g_norm, ys)


def _positions_kernel(idx_ref, start_ref, o_ref):
    idx = idx_ref[...]
    lane = lax.broadcasted_iota(jnp.int32, idx.shape, 1)
    start = start_ref[...]

    def column(c):
        return jnp.sum(jnp.where(lane == c, idx, 0), axis=-1, keepdims=True)

    def seg_start(e):
        return jnp.sum(jnp.where(lane == e, start, 0), axis=-1, keepdims=True)

    pos1 = seg_start(column(0)) + column(2)
    pos2 = seg_start(column(1)) + column(3)
    o_ref[...] = jnp.where(lane == 0, pos1, jnp.where(lane == 1, pos2, 0))


def _positions(idx, seg_start, tb):
    s = idx.shape[0]
    start_row = jnp.pad(seg_start, (0, ROUTER_LANES - N_EXPERTS)).reshape(1, ROUTER_LANES)
    out = pl.pallas_call(
        _positions_kernel,
        grid=(s // tb,),
        in_specs=[pl.BlockSpec((tb, ROUTER_LANES), lambda i: (i, 0)),
                  pl.BlockSpec((1, ROUTER_LANES), lambda i: (0, 0))],
        out_specs=pl.BlockSpec((tb, ROUTER_LANES), lambda i: (i, 0)),
        out_shape=jax.ShapeDtypeStruct((s, ROUTER_LANES), jnp.int32),
        compiler_params=_params("parallel"),
        name="moe_positions",
    )(idx, start_row)
    return jnp.concatenate([out[:, 0], out[:, 1]])


def _count_le(sorted_vals, queries):
    return jnp.sum(sorted_vals[None, :] <= queries[:, None], axis=1).astype(jnp.int32)


def _moe(h, g, g_next, final, w_group, b_group, w_expert, b_expert, w_gate, w_up, w_down):
    s, d = h.shape
    pad = ROUTER_LANES - N_GROUPS - N_EXPERTS
    w_r = jnp.pad(jnp.concatenate([w_group, w_expert], axis=1), ((0, 0), (0, pad)))
    w_hi = w_r.astype(BF16)
    w_lo = (w_r - w_hi.astype(F32)).astype(BF16)
    b_r = jnp.pad(jnp.concatenate([b_group, b_expert]), (0, pad)).reshape(1, ROUTER_LANES)
    idx, gates, counts = _router(h, g, w_hi, w_lo, b_r.astype(F32), tm=min(512, s))

    tm = EXPERT_TILE
    n_tiles = 2 * s // tm
    n_pairs = n_tiles + N_EXPERTS - 1
    counts = counts[0, N_GROUPS:N_GROUPS + N_EXPERTS].astype(jnp.int32)
    seg_end = jnp.cumsum(counts)
    seg_start = seg_end - counts
    pos = _positions(idx, seg_start, tb=min(2048, s))
    tile_row = jnp.arange(n_tiles, dtype=jnp.int32) * tm
    first_e = _count_le(seg_end, tile_row)
    last_e = _count_le(seg_end, tile_row + tm - 1)
    per_tile = last_e - first_e + 1
    pair_end = jnp.cumsum(per_tile)
    pair_start = pair_end - per_tile
    total = pair_end[-1]
    p = jnp.minimum(jnp.arange(n_pairs, dtype=jnp.int32), total - 1)
    p_tile = _count_le(pair_end, p)
    p_expert = first_e[p_tile] + p - pair_start[p_tile]
    p_lo = jnp.clip(seg_start[p_expert] - p_tile * tm, 0, tm)
    p_hi = jnp.clip(seg_end[p_expert] - p_tile * tm, 0, tm)
    p_first = (p_expert == first_e[p_tile]).astype(jnp.int32)

    xs = _dispatch(pos, h, g, tb=min(256, s))
    ys = _experts((p_tile, p_expert, p_lo, p_hi, p_first, total.reshape(1)), xs,
                  w_gate, w_up, w_down)
    return _combine(pos, h, gates, g_next, ys, tb=min(256, s), final=final)


def _rope_tables(s):
    inv_freq = ROPE_THETA ** (-jnp.arange(0, MLA_ROPE_DIM, 2, dtype=F32) / MLA_ROPE_DIM)
    ang = jnp.arange(s, dtype=F32)[:, None] * inv_freq[None, :]
    cos, sin = jnp.cos(ang), jnp.sin(ang)
    zeros = jnp.zeros((s, 128 - MLA_ROPE_DIM), F32)
    return (jnp.concatenate([cos, cos, zeros], axis=1), jnp.concatenate([sin, sin, zeros], axis=1))


def _rot_half_cols(w):
    half = MLA_ROPE_DIM // 2
    return jnp.concatenate([-w[..., half:], w[..., :half]], axis=-1)


def _even_mixer(h, hn, w_in, pool_w, pool_scale, q_norm, w_uq, kv_norm, w_ukv, w_out, ck, sk,
                expert_weights):
    s, d = h.shape
    w_kpe = w_in[:, POOL_DIM + MLA_Q_RANK + MLA_KV_RANK:]
    w_in_ext = jnp.concatenate([w_in, _rot_half_cols(w_kpe)], axis=1).astype(BF16)
    big = min(1024, s)
    z = _matmul([hn], [w_in_ext], BF16, tm=big, tn=896, name="even_in_proj")

    ya = _pool_mixer(z, pool_w.astype(BF16), pool_scale.reshape(1, POOL_DIM), ts=min(512, s))

    w_q = w_uq.reshape(MLA_Q_RANK, MLA_HEADS, MLA_QK_DIM)
    w_q = jnp.concatenate([w_q, _rot_half_cols(w_q[..., MLA_NOPE_DIM:])], axis=-1)
    w_q = jnp.transpose(w_q, (1, 0, 2)).astype(BF16)
    w_kv = w_ukv.reshape(MLA_KV_RANK, MLA_HEADS // 2, 2, MLA_NOPE_DIM + MLA_V_DIM)
    w_k = jnp.transpose(w_kv[..., :MLA_NOPE_DIM], (1, 0, 2, 3))
    w_k = w_k.reshape(MLA_HEADS // 2, MLA_KV_RANK, 2 * MLA_NOPE_DIM).astype(BF16)
    w_vt = jnp.transpose(w_kv[..., MLA_NOPE_DIM:], (1, 2, 3, 0))
    w_vt = w_vt.reshape(MLA_HEADS // 2, 2 * MLA_V_DIM, MLA_KV_RANK).astype(BF16)
    q = _mla_q(z, q_norm.reshape(1, -1), w_q, ck, sk, tm=big, hb=4)
    k, vt = _mla_kv(z, kv_norm.reshape(1, -1), w_k, w_vt, ck, sk, tm=big, hb=4)
    yb, casts = _mla_attention(q, k, vt, tb=min(512, s), n_sub=4 if s >= 2048 else 1,
                               side=expert_weights)

    w_o = w_out.astype(BF16)
    h = _matmul([ya, yb], [w_o[:POOL_DIM], w_o[POOL_DIM:]], F32, tm=min(512, s), tn=1024,
                residual=h, name="even_out_proj")
    return h, casts


def _odd_mixer(h, hn, w_in, ln_g, ln_b, w_s, b_s, sinks, rel_bias, w_out, expert_weights):
    s, d = h.shape
    big = min(1024, s)
    z = _matmul([hn], [w_in.astype(BF16)], BF16, tm=big, tn=1024, name="odd_in_proj")
    yc = _sgu_mixer(z, ln_g.reshape(1, -1), ln_b.reshape(1, -1), w_s, jnp.transpose(b_s),
                    ts=min(256, s))
    yd, cast_a = _swa_mixer(z, sinks, _swa_bias(rel_bias), side=expert_weights[:1])
    w_o = w_out.astype(BF16)
    h, cast_b = _matmul([yc, yd], [w_o[:SGU_DIM], w_o[SGU_DIM:]], F32, tm=min(512, s), tn=1024,
                        residual=h, name="odd_out_proj", side=expert_weights[1:2])
    return h, cast_a + cast_b


def kernel(x, mem, norm_mix, norm_xattn, norm_ffn, norm_mem, final_norm, rel_bias, e_w_in, pool_w, pool_scale, mla_q_norm, mla_w_uq, mla_kv_norm, mla_w_ukv, e_w_out, o_w_in, sgu_ln_g, sgu_ln_b, sgu_w, sgu_b, swa_sinks, o_w_out, xa_wq, xa_wk, xa_wv, xa_wo, moe_w_group, moe_b_group, moe_w_expert, moe_b_expert, moe_w_gate, moe_w_up, moe_w_down):
    batch, s, d = x.shape
    assert batch == 1
    depth = norm_mix.shape[0]
    ck, sk = _rope_tables(s)
    h = x.reshape(s, d)
    mem2 = mem.reshape(mem.shape[1], d)
    hn = _rmsnorm(h, norm_mix[0], BF16, tm=min(512, s))
    for layer in range(depth):
        i = layer // 2
        last = layer == depth - 1
        expert_weights = [(moe_w_gate, layer), (moe_w_up, layer), (moe_w_down, layer)]
        if layer % 2 == 0:
            h, casts = _even_mixer(h, hn, e_w_in[i], pool_w[i], pool_scale[i], mla_q_norm[i],
                                   mla_w_uq[i], mla_kv_norm[i], mla_w_ukv[i], e_w_out[i], ck, sk,
                                   expert_weights)
        else:
            h, casts = _odd_mixer(h, hn, o_w_in[i], sgu_ln_g[i], sgu_ln_b[i], sgu_w[i], sgu_b[i],
                                  swa_sinks[i], rel_bias, o_w_out[i], expert_weights)
        mem_n = _rmsnorm(mem2, norm_mem[layer], BF16, tm=mem2.shape[0])
        w_kv_mem = jnp.concatenate([xa_wk[layer], xa_wv[layer]], axis=1).astype(BF16)
        kv_mem = _matmul([mem_n], [w_kv_mem], BF16, tm=mem2.shape[0], tn=2 * XATTN_DIM,
                         name="mem_kv_proj")
        h, more_casts = _xattn(h, norm_xattn[layer].reshape(1, d), xa_wq[layer].astype(BF16),
                               kv_mem, xa_wo[layer].astype(BF16), tm=min(256, s),
                               side=expert_weights[len(casts):])
        w_gate, w_up, w_down = casts + more_casts
        g_next = final_norm if last else norm_mix[layer + 1]
        res = _moe(h, norm_ffn[layer].reshape(1, d), g_next.reshape(1, d), last,
                   moe_w_group[layer], moe_b_group[layer], moe_w_expert[layer],
                   moe_b_expert[layer], w_gate, w_up, w_down)
        if last:
            out, = res
        else:
            h, hn = res
    return out.reshape(batch, s, d)
```

```python
import functools
import math

import jax
import jax.numpy as jnp
from jax import lax
from jax.experimental import pallas as pl
from jax.experimental.pallas import tpu as pltpu

F32 = jnp.float32
BF16 = jnp.bfloat16

NORM_EPS = 1e-6
NEG_INF = -1e30

POOL_WINDOWS = (2, 4, 8, 16)
POOL_GROUP_DIM = 256
POOL_DIM = 1024
POOL_HALO = 16

MLA_HEADS = 24
MLA_Q_RANK = 1024
MLA_KV_RANK = 512
MLA_NOPE_DIM = 128
MLA_ROPE_DIM = 64
MLA_V_DIM = 128
MLA_QK_DIM = MLA_NOPE_DIM + MLA_ROPE_DIM
ROPE_THETA = 10000.0

SGU_GROUPS = 8
SGU_GROUP_DIM = 256
SGU_DIM = 2048
SGU_CHUNK = 128

SWA_HEADS = 32
SWA_KV_HEADS = 8
SWA_HEAD_DIM = 64
SWA_BLOCK = 128
REL_BUCKETS = 32
REL_MAX_DIST = 128

XATTN_HEADS = 4
XATTN_HEAD_DIM = 128
XATTN_DIM = 512

N_GROUPS = 4
EXPERTS_PER_GROUP = 8
N_EXPERTS = 32
EXPERT_FF = 512
ROUTER_LANES = 128
EXPERT_TILE = 256
DMA_ISSUE_UNROLL = 8
ROW_CHUNK = 16
SIDE_CAST_BLOCKS = 128

VMEM_LIMIT_BYTES = 56 * 1024 * 1024


def _params(*semantics):
    return pltpu.CompilerParams(dimension_semantics=semantics, vmem_limit_bytes=VMEM_LIMIT_BYTES)


def _rms(x, g):
    return x * lax.rsqrt(jnp.mean(x * x, axis=-1, keepdims=True) + NORM_EPS) * g


def _rmsnorm_kernel(x_ref, g_ref, o_ref):
    o_ref[...] = _rms(x_ref[...].astype(F32), g_ref[...]).astype(o_ref.dtype)


def _rmsnorm(x, g, out_dtype, tm):
    m, d = x.shape
    return pl.pallas_call(
        _rmsnorm_kernel,
        grid=(m // tm,),
        in_specs=[pl.BlockSpec((tm, d), lambda i: (i, 0)), pl.BlockSpec((1, d), lambda i: (0, 0))],
        out_specs=pl.BlockSpec((tm, d), lambda i: (i, 0)),
        out_shape=jax.ShapeDtypeStruct((m, d), out_dtype),
        compiler_params=_params("parallel"),
        name="rmsnorm",
    )(x, g.reshape(1, d).astype(F32))


def _side_cast_plan(w, layer, host_steps, step_of):
    depth, e, a, b = w.shape
    n_blocks = SIDE_CAST_BLOCKS
    while n_blocks > host_steps:
        n_blocks //= 2
    rows = e * a // n_blocks

    def block_of(*idx):
        return jnp.minimum(step_of(*idx), n_blocks - 1)

    return dict(
        operand=w.reshape(depth, e * a, b),
        in_spec=pl.BlockSpec((None, rows, b), lambda *idx: (layer, block_of(*idx), 0)),
        out_spec=pl.BlockSpec((rows, b), lambda *idx: (block_of(*idx), 0)),
        out_shape=jax.ShapeDtypeStruct((e * a, b), BF16),
        shape=(e, a, b),
    )


def _host_call(body, side, operands, *, grid, in_specs, out_spec, out_shape, semantics, name,
               step_of, scratch_shapes=()):
    plans = [_side_cast_plan(w, layer, math.prod(grid), step_of) for w, layer in side]
    n_in, n_side = len(in_specs), len(plans)

    def wrapped(*refs):
        side_in = refs[n_in:n_in + n_side]
        side_out = refs[n_in + n_side + 1:n_in + 2 * n_side + 1]
        for src, dst in zip(side_in, side_out):
            dst[...] = src[...].astype(dst.dtype)
        body(*refs[:n_in], refs[n_in + n_side], *refs[n_in + 2 * n_side + 1:])

    res = pl.pallas_call(
        wrapped,
        grid=grid,
        in_specs=list(in_specs) + [p["in_spec"] for p in plans],
        out_specs=[out_spec] + [p["out_spec"] for p in plans],
        out_shape=[out_shape] + [p["out_shape"] for p in plans],
        scratch_shapes=list(scratch_shapes),
        compiler_params=_params(*semantics),
        name=name,
    )(*operands, *[p["operand"] for p in plans])
    return res[0], [r.reshape(p["shape"]) for r, p in zip(res[1:], plans)]


def _matmul_kernel(*refs, n_in, has_res):
    o_ref = refs[-1]
    acc = jnp.dot(refs[0][...], refs[n_in][...], preferred_element_type=F32)
    for k in range(1, n_in):
        acc += jnp.dot(refs[k][...], refs[n_in + k][...], preferred_element_type=F32)
    if has_res:
        acc += refs[2 * n_in][...]
    o_ref[...] = acc.astype(o_ref.dtype)


def _matmul(xs, ws, out_dtype, tm, tn, residual=None, name="matmul", side=()):
    m = xs[0].shape[0]
    n = ws[0].shape[1]
    in_specs = [pl.BlockSpec((tm, x.shape[1]), lambda i, j: (i, 0)) for x in xs]
    in_specs += [pl.BlockSpec((w.shape[0], tn), lambda i, j: (0, j)) for w in ws]
    args = list(xs) + list(ws)
    if residual is not None:
        in_specs.append(pl.BlockSpec((tm, tn), lambda i, j: (i, j)))
        args.append(residual)
    n_j = n // tn
    out, casts = _host_call(
        functools.partial(_matmul_kernel, n_in=len(xs), has_res=residual is not None),
        side, args,
        grid=(m // tm, n_j),
        in_specs=in_specs,
        out_spec=pl.BlockSpec((tm, tn), lambda i, j: (i, j)),
        out_shape=jax.ShapeDtypeStruct((m, n), out_dtype),
        semantics=("arbitrary", "arbitrary") if side else ("parallel", "parallel"),
        name=name,
        step_of=lambda i, j: i * n_j + j,
    )
    return (out, casts) if side else out


def _pool_kernel(a_ref, halo_ref, w_ref, scale_ref, o_ref, *, ts):
    i = pl.program_id(0)
    g = pl.program_id(1)
    win = jnp.left_shift(2, g)
    a = a_ref[...]
    halo = jnp.where(i > 0, halo_ref[...], jnp.zeros_like(halo_ref))
    ext = jnp.concatenate([halo, a], axis=0)
    row = lax.broadcasted_iota(jnp.int32, (ts, POOL_HALO + ts), 0) + POOL_HALO
    col = lax.broadcasted_iota(jnp.int32, (ts, POOL_HALO + ts), 1)
    band = ((col <= row) & (col > row - win)).astype(BF16)
    wsum = jnp.dot(band, ext, preferred_element_type=F32)
    t = i * ts + lax.broadcasted_iota(jnp.int32, (ts, 1), 0)
    count = jnp.minimum(t + 1, win).astype(F32)
    d = (wsum / count - a.astype(F32)).astype(BF16)
    y = jnp.dot(d, w_ref[0], preferred_element_type=F32) * scale_ref[...]
    o_ref[...] = y.astype(o_ref.dtype)


def _pool_mixer(z, pool_w, pool_scale, ts):
    s = z.shape[0]
    c = POOL_GROUP_DIM
    hb = ts // POOL_HALO
    return pl.pallas_call(
        functools.partial(_pool_kernel, ts=ts),
        grid=(s // ts, len(POOL_WINDOWS)),
        in_specs=[
            pl.BlockSpec((ts, c), lambda i, g: (i, g)),
            pl.BlockSpec((POOL_HALO, c), lambda i, g: (jnp.maximum(i * hb - 1, 0), g)),
            pl.BlockSpec((1, c, c), lambda i, g: (g, 0, 0)),
            pl.BlockSpec((1, c), lambda i, g: (0, g)),
        ],
        out_specs=pl.BlockSpec((ts, c), lambda i, g: (i, g)),
        out_shape=jax.ShapeDtypeStruct((s, POOL_DIM), BF16),
        compiler_params=_params("parallel", "parallel"),
        name="pool_mixer",
    )(z, z, pool_w, pool_scale)


def _rope_half_block(blk, ck, sk):
    return blk * ck + pltpu.roll(blk, 64, 1) * sk


def _mla_q_kernel(cq_ref, g_ref, w_ref, ck_ref, sk_ref, o_ref, xn_ref, *, hb, scale):
    @pl.when(pl.program_id(1) == 0)
    def _():
        xn_ref[...] = _rms(cq_ref[...].astype(F32), g_ref[...]).astype(BF16)

    xn = xn_ref[...]
    for hh in range(hb):
        q = jnp.dot(xn, w_ref[hh], preferred_element_type=F32)
        o_ref[hh, :, 0:MLA_NOPE_DIM] = (q[:, :MLA_NOPE_DIM] * scale).astype(o_ref.dtype)
        pe = _rope_half_block(q[:, MLA_NOPE_DIM:], ck_ref[...], sk_ref[...]) * scale
        o_ref[hh, :, MLA_NOPE_DIM:MLA_QK_DIM] = pe[:, :MLA_ROPE_DIM].astype(o_ref.dtype)


def _mla_q(z, q_norm, w_q, ck, sk, tm, hb):
    s = z.shape[0]
    scale = MLA_QK_DIM ** -0.5 * math.log2(math.e)
    return pl.pallas_call(
        functools.partial(_mla_q_kernel, hb=hb, scale=scale),
        grid=(s // tm, MLA_HEADS // hb),
        in_specs=[
            pl.BlockSpec((tm, MLA_Q_RANK), lambda i, j: (i, POOL_DIM // MLA_Q_RANK)),
            pl.BlockSpec((1, MLA_Q_RANK), lambda i, j: (0, 0)),
            pl.BlockSpec((hb, MLA_Q_RANK, 256), lambda i, j: (j, 0, 0)),
            pl.BlockSpec((tm, 128), lambda i, j: (i, 0)),
            pl.BlockSpec((tm, 128), lambda i, j: (i, 0)),
        ],
        out_specs=pl.BlockSpec((hb, tm, MLA_QK_DIM), lambda i, j: (j, i, 0)),
        out_shape=jax.ShapeDtypeStruct((MLA_HEADS, s, MLA_QK_DIM), BF16),
        scratch_shapes=[pltpu.VMEM((tm, MLA_Q_RANK), BF16)],
        compiler_params=_params("parallel", "arbitrary"),
        name="mla_q_proj",
    )(z, q_norm, w_q, ck, sk)


def _mla_kv_kernel(ckv_ref, kpe_ref, g_ref, wk_ref, wvt_ref, ck_ref, sk_ref, k_ref, vt_ref,
                   xn_ref, kr_ref, *, hb):
    @pl.when(pl.program_id(1) == 0)
    def _():
        xn_ref[...] = _rms(ckv_ref[...].astype(F32), g_ref[...]).astype(BF16)
        kr = _rope_half_block(kpe_ref[...].astype(F32), ck_ref[...], sk_ref[...])
        kr_ref[...] = kr.astype(BF16)

    xn = xn_ref[...]
    for pair in range(hb // 2):
        kk = jnp.dot(xn, wk_ref[pair], preferred_element_type=F32)
        vv = lax.dot_general(wvt_ref[pair], xn, (((1,), (1,)), ((), ())),
                             preferred_element_type=F32)
        for side in range(2):
            hh = 2 * pair + side
            k_ref[hh, :, 0:MLA_NOPE_DIM] = (
                kk[:, side * MLA_NOPE_DIM:(side + 1) * MLA_NOPE_DIM].astype(k_ref.dtype))
            k_ref[hh, :, MLA_NOPE_DIM:MLA_QK_DIM] = kr_ref[:, 0:MLA_ROPE_DIM]
            vt_ref[hh] = vv[side * MLA_V_DIM:(side + 1) * MLA_V_DIM, :].astype(vt_ref.dtype)


def _mla_kv(z, kv_norm, w_k, w_vt, ck, sk, tm, hb):
    s = z.shape[0]
    ckv_block = (POOL_DIM + MLA_Q_RANK) // MLA_KV_RANK
    kpe_block = (POOL_DIM + MLA_Q_RANK + MLA_KV_RANK) // 128
    return pl.pallas_call(
        functools.partial(_mla_kv_kernel, hb=hb),
        grid=(s // tm, MLA_HEADS // hb),
        in_specs=[
            pl.BlockSpec((tm, MLA_KV_RANK), lambda i, j: (i, ckv_block)),
            pl.BlockSpec((tm, 128), lambda i, j: (i, kpe_block)),
            pl.BlockSpec((1, MLA_KV_RANK), lambda i, j: (0, 0)),
            pl.BlockSpec((hb // 2, MLA_KV_RANK, 2 * MLA_NOPE_DIM), lambda i, j: (j, 0, 0)),
            pl.BlockSpec((hb // 2, 2 * MLA_V_DIM, MLA_KV_RANK), lambda i, j: (j, 0, 0)),
            pl.BlockSpec((tm, 128), lambda i, j: (i, 0)),
            pl.BlockSpec((tm, 128), lambda i, j: (i, 0)),
        ],
        out_specs=[
            pl.BlockSpec((hb, tm, MLA_QK_DIM), lambda i, j: (j, i, 0)),
            pl.BlockSpec((hb, MLA_V_DIM, tm), lambda i, j: (j, 0, i)),
        ],
        out_shape=[
            jax.ShapeDtypeStruct((MLA_HEADS, s, MLA_QK_DIM), BF16),
            jax.ShapeDtypeStruct((MLA_HEADS, MLA_V_DIM, s), BF16),
        ],
        scratch_shapes=[pltpu.VMEM((tm, MLA_KV_RANK), BF16), pltpu.VMEM((tm, 128), BF16)],
        compiler_params=_params("parallel", "arbitrary"),
        name="mla_kv_proj",
    )(z, z, kv_norm, w_k, w_vt, ck, sk)


def _mla_attn_kernel(q_ref, k_ref, v_ref, o_ref, m_ref, l_ref, acc_ref, s_ref, *, tb, n_sub):
    i = pl.program_id(1)
    m_ref[...] = jnp.full_like(m_ref, NEG_INF)
    l_ref[...] = jnp.zeros_like(l_ref)
    acc_ref[...] = jnp.zeros_like(acc_ref)

    def scores(x, kv):
        start = pl.multiple_of(kv * tb, tb)
        q = q_ref[0, x * tb:(x + 1) * tb, :]
        k = k_ref[0, pl.ds(start, tb), :]
        return lax.dot_general(k, q, (((1,), (1,)), ((), ())), preferred_element_type=F32)

    def update(x, kv, s, masked):
        start = pl.multiple_of(kv * tb, tb)
        vt = v_ref[0, :, pl.ds(start, tb)]
        if masked:
            key = lax.broadcasted_iota(jnp.int32, (tb, tb), 0)
            qry = lax.broadcasted_iota(jnp.int32, (tb, tb), 1)
            s = jnp.where(key <= qry, s, NEG_INF)
        m_prev = m_ref[x]
        m_new = jnp.maximum(m_prev, jnp.max(s, axis=0, keepdims=True))
        alpha = jnp.exp2(m_prev - m_new)
        p = jnp.exp2(s - m_new)
        l_ref[x] = alpha * l_ref[x] + jnp.sum(p, axis=0, keepdims=True)
        acc_ref[x] = alpha * acc_ref[x] + jnp.dot(vt, p.astype(BF16), preferred_element_type=F32)
        m_ref[x] = m_new

    for x in range(n_sub):
        s_ref[x] = scores(x, 0)

    def body(kv, carry):
        for x in range(n_sub):
            s = s_ref[x]
            s_ref[x] = scores(x, kv + 1)
            update(x, kv, s, False)
        return carry

    first_diag = n_sub * i
    lax.fori_loop(0, first_diag, body, 0)
    for d in range(n_sub):
        for x in range(d, n_sub):
            s = s_ref[x] if d == 0 else scores(x, first_diag + d)
            update(x, first_diag + d, s, x == d)
    for x in range(n_sub):
        o_ref[x * tb:(x + 1) * tb, :] = jnp.transpose(acc_ref[x] / l_ref[x]).astype(o_ref.dtype)


def _mla_attention(q, k, v, tb, n_sub, side=()):
    s = q.shape[1]
    tq = tb * n_sub
    n_q = s // tq
    return _host_call(
        functools.partial(_mla_attn_kernel, tb=tb, n_sub=n_sub),
        side, (q, k, v),
        grid=(MLA_HEADS, n_q),
        in_specs=[
            pl.BlockSpec((1, tq, MLA_QK_DIM), lambda h, i: (h, i, 0)),
            pl.BlockSpec((1, s, MLA_QK_DIM), lambda h, i: (h, 0, 0)),
            pl.BlockSpec((1, MLA_V_DIM, s), lambda h, i: (h, 0, 0)),
        ],
        out_spec=pl.BlockSpec((tq, MLA_V_DIM), lambda h, i: (i, h)),
        out_shape=jax.ShapeDtypeStruct((s, MLA_HEADS * MLA_V_DIM), BF16),
        scratch_shapes=[pltpu.VMEM((n_sub, 1, tb), F32), pltpu.VMEM((n_sub, 1, tb), F32),
                        pltpu.VMEM((n_sub, MLA_V_DIM, tb), F32),
                        pltpu.VMEM((n_sub, tb, tb), F32)],
        semantics=("arbitrary", "arbitrary"),
        name="mla_attention",
        step_of=lambda h, i: h * n_q + i,
    )


def _gelu_tanh(x):
    return 0.5 * x * (1.0 + jnp.tanh(math.sqrt(2.0 / math.pi) * (x + 0.044715 * (x * x * x))))


def _sgu_kernel(u_ref, v_ref, g_ref, b_ref, w_ref, bs_ref, o_ref, *, ts):
    row = lax.broadcasted_iota(jnp.int32, (SGU_CHUNK, SGU_CHUNK), 0)
    col = lax.broadcasted_iota(jnp.int32, (SGU_CHUNK, SGU_CHUNK), 1)
    causal = col <= row
    for c in range(ts // SGU_CHUNK):
        rows = slice(c * SGU_CHUNK, (c + 1) * SGU_CHUNK)
        v = _gelu_tanh(v_ref[rows, :].astype(F32))
        mu = jnp.mean(v, axis=-1, keepdims=True)
        vc = v - mu
        vn = vc * lax.rsqrt(jnp.mean(vc * vc, axis=-1, keepdims=True) + NORM_EPS)
        vn = (vn * g_ref[...] + b_ref[...]).astype(BF16)
        for g in range(SGU_GROUPS):
            cols = slice(g * SGU_GROUP_DIM, (g + 1) * SGU_GROUP_DIM)
            w = jnp.where(causal, w_ref[g], 0.0).astype(BF16)
            mixed = jnp.dot(w, vn[:, cols], preferred_element_type=F32) + bs_ref[:, g:g + 1]
            u = _gelu_tanh(u_ref[rows, cols].astype(F32))
            o_ref[rows, cols] = (u * mixed).astype(o_ref.dtype)


def _sgu_mixer(z, ln_g, ln_b, w_s, b_s_t, ts):
    s = z.shape[0]
    return pl.pallas_call(
        functools.partial(_sgu_kernel, ts=ts),
        grid=(s // ts,),
        in_specs=[
            pl.BlockSpec((ts, SGU_DIM), lambda i: (i, 0)),
            pl.BlockSpec((ts, SGU_DIM), lambda i: (i, 1)),
            pl.BlockSpec((1, SGU_DIM), lambda i: (0, 0)),
            pl.BlockSpec((1, SGU_DIM), lambda i: (0, 0)),
            pl.BlockSpec((SGU_GROUPS, SGU_CHUNK, SGU_CHUNK), lambda i: (0, 0, 0)),
            pl.BlockSpec((SGU_CHUNK, SGU_GROUPS), lambda i: (0, 0)),
        ],
        out_specs=pl.BlockSpec((ts, SGU_DIM), lambda i: (i, 0)),
        out_shape=jax.ShapeDtypeStruct((s, SGU_DIM), BF16),
        compiler_params=_params("parallel"),
        name="sgu_mixer",
    )(z, z, ln_g, ln_b, w_s, b_s_t)


def _swa_kernel(sink_ref, q_ref, kp_ref, kc_ref, vp_ref, vc_ref, bias_ref, o_ref):
    n = pl.program_id(0)
    blk = SWA_BLOCK
    hd = SWA_HEAD_DIM
    col = lax.broadcasted_iota(jnp.int32, (blk, 2 * blk), 1)
    no_prev = (n == 0) & (col < blk)
    q_low = lax.broadcasted_iota(jnp.int32, (blk, 2 * hd), 1) < hd
    kv_low = lax.broadcasted_iota(jnp.int32, (2 * blk, 2 * hd), 1) < hd
    scale = hd ** -0.5
    for pair in range(SWA_KV_HEADS // 2):
        cols = slice(pair * 2 * hd, (pair + 1) * 2 * hd)
        kp = jnp.concatenate([kp_ref[:, cols], kc_ref[:, cols]], axis=0).astype(F32)
        vp = jnp.concatenate([vp_ref[:, cols], vc_ref[:, cols]], axis=0).astype(F32)
        k_swapped = pltpu.roll(kp, hd, 1)
        v_swapped = pltpu.roll(vp, hd, 1)
        for half in range(2):
            kvh = 2 * pair + half
            own = kv_low if half == 0 else ~kv_low
            k_dup = jnp.where(own, kp, k_swapped).astype(BF16)
            v_dup = jnp.where(own, vp, v_swapped)
            v_side = (jnp.where(kv_low, v_dup, 0.0).astype(BF16),
                      jnp.where(kv_low, 0.0, v_dup).astype(BF16))
            for t in range(2):
                tile = 2 * kvh + t
                qt = q_ref[:, tile * 2 * hd:(tile + 1) * 2 * hd].astype(F32) * scale
                o = None
                for side in range(2):
                    h = 2 * tile + side
                    qh = jnp.where(q_low if side == 0 else ~q_low, qt, 0.0).astype(BF16)
                    s = lax.dot_general(qh, k_dup, (((1,), (1,)), ((), ())),
                                        preferred_element_type=F32)
                    s = jnp.where(no_prev, NEG_INF, s + bias_ref[h])
                    sink = sink_ref[h]
                    m = jnp.maximum(jnp.max(s, axis=-1, keepdims=True), sink)
                    p = jnp.exp(s - m)
                    denom = jnp.sum(p, axis=-1, keepdims=True) + jnp.exp(sink - m)
                    part = jnp.dot(p.astype(BF16), v_side[side],
                                   preferred_element_type=F32) / denom
                    o = part if o is None else o + part
                o_ref[:, tile * 2 * hd:(tile + 1) * 2 * hd] = o.astype(o_ref.dtype)


def _swa_mixer(z, sinks, bias, side=()):
    s = z.shape[0]
    blk = SWA_BLOCK
    qw = SWA_HEADS * SWA_HEAD_DIM
    kw = SWA_KV_HEADS * SWA_HEAD_DIM
    q_block = 2 * SGU_DIM // qw
    k_block = (2 * SGU_DIM + qw) // kw
    v_block = k_block + 1
    prev = lambda n: jnp.maximum(n - 1, 0)
    return _host_call(
        _swa_kernel,
        side, (sinks, z, z, z, z, z, bias),
        grid=(s // blk,),
        in_specs=[
            pl.BlockSpec(memory_space=pltpu.SMEM),
            pl.BlockSpec((blk, qw), lambda n: (n, q_block)),
            pl.BlockSpec((blk, kw), lambda n: (prev(n), k_block)),
            pl.BlockSpec((blk, kw), lambda n: (n, k_block)),
            pl.BlockSpec((blk, kw), lambda n: (prev(n), v_block)),
            pl.BlockSpec((blk, kw), lambda n: (n, v_block)),
            pl.BlockSpec((SWA_HEADS, blk, 2 * blk), lambda n: (0, 0, 0)),
        ],
        out_spec=pl.BlockSpec((blk, qw), lambda n: (n, 0)),
        out_shape=jax.ShapeDtypeStruct((s, qw), BF16),
        semantics=("arbitrary",),
        name="swa_mixer",
        step_of=lambda n: n,
    )


def _t5_bucket(dist):
    max_exact = REL_BUCKETS // 2
    large = max_exact + (jnp.log(jnp.maximum(dist, 1).astype(F32) / max_exact)
                         / math.log(REL_MAX_DIST / max_exact) * (REL_BUCKETS - max_exact)).astype(jnp.int32)
    large = jnp.minimum(large, REL_BUCKETS - 1)
    return jnp.where(dist < max_exact, dist, large)


def _swa_bias(rel_bias):
    blk = SWA_BLOCK
    by_dist = rel_bias[_t5_bucket(jnp.arange(blk))].astype(F32)
    outside = jnp.full_like(by_dist, NEG_INF)
    diag = jnp.concatenate([outside, by_dist[::-1], outside], axis=0)
    period = 3 * blk
    diag = jnp.roll(diag, -(blk - 1), axis=0)
    skew = jnp.tile(diag, (blk, 1))[:blk * (period - 1)].reshape(blk, period - 1, SWA_HEADS)
    return jnp.transpose(skew[:, :2 * blk], (2, 0, 1))


def _xattn_kernel(h_ref, g_ref, wq_ref, kv_ref, wo_ref, o_ref):
    x = h_ref[...]
    xn = _rms(x, g_ref[...]).astype(BF16)
    q = jnp.dot(xn, wq_ref[...], preferred_element_type=F32) * (XATTN_HEAD_DIM ** -0.5)
    q = q.astype(BF16)
    outs = []
    for hd in range(XATTN_HEADS):
        cols = slice(hd * XATTN_HEAD_DIM, (hd + 1) * XATTN_HEAD_DIM)
        k = kv_ref[:, cols]
        v = kv_ref[:, XATTN_DIM + hd * XATTN_HEAD_DIM:XATTN_DIM + (hd + 1) * XATTN_HEAD_DIM]
        s = lax.dot_general(q[:, cols], k, (((1,), (1,)), ((), ())), preferred_element_type=F32)
        p = jnp.exp(s - jnp.max(s, axis=-1, keepdims=True))
        denom = jnp.sum(p, axis=-1, keepdims=True)
        o = jnp.dot(p.astype(BF16), v, preferred_element_type=F32) / denom
        outs.append(o.astype(BF16))
    o = jnp.concatenate(outs, axis=-1)
    o_ref[...] = x + jnp.dot(o, wo_ref[...], preferred_element_type=F32)


def _xattn(h, g, wq, kv_mem, wo, tm, side=()):
    s, d = h.shape
    mlen = kv_mem.shape[0]
    return _host_call(
        _xattn_kernel,
        side, (h, g, wq, kv_mem, wo),
        grid=(s // tm,),
        in_specs=[
            pl.BlockSpec((tm, d), lambda i: (i, 0)),
            pl.BlockSpec((1, d), lambda i: (0, 0)),
            pl.BlockSpec((d, XATTN_DIM), lambda i: (0, 0)),
            pl.BlockSpec((mlen, 2 * XATTN_DIM), lambda i: (0, 0)),
            pl.BlockSpec((XATTN_DIM, d), lambda i: (0, 0)),
        ],
        out_spec=pl.BlockSpec((tm, d), lambda i: (i, 0)),
        out_shape=jax.ShapeDtypeStruct((s, d), F32),
        semantics=("arbitrary",),
        name="memory_xattn",
        step_of=lambda i: i,
    )


def _router_kernel(h_ref, g_ref, whi_ref, wlo_ref, b_ref, idx_ref, gate_ref, cnt_ref, carry_ref,
                   *, tm):
    @pl.when(pl.program_id(0) == 0)
    def _():
        carry_ref[...] = jnp.zeros_like(carry_ref)

    xn = _rms(h_ref[...], g_ref[...])
    x_hi = xn.astype(BF16)
    x_lo = (xn - x_hi.astype(F32)).astype(BF16)
    logits = (jnp.dot(x_hi, whi_ref[...], preferred_element_type=F32)
              + jnp.dot(x_lo, whi_ref[...], preferred_element_type=F32)
              + jnp.dot(x_hi, wlo_ref[...], preferred_element_type=F32)) + b_ref[...]

    lane = lax.broadcasted_iota(jnp.int32, (tm, ROUTER_LANES), 1)
    big = jnp.int32(ROUTER_LANES)

    def first_argmax(vals):
        top = jnp.max(vals, axis=-1, keepdims=True)
        idx = jnp.min(jnp.where(vals == top, lane, big), axis=-1, keepdims=True)
        return top, idx

    is_group = lane < N_GROUPS
    g_logits = jnp.where(is_group, logits, NEG_INF)
    g_top, g_idx = first_argmax(g_logits)
    p_group = 1.0 / jnp.sum(jnp.where(is_group, jnp.exp(g_logits - g_top), 0.0), axis=-1,
                            keepdims=True)
    lo = N_GROUPS + g_idx * EXPERTS_PER_GROUP
    in_group = (lane >= lo) & (lane < lo + EXPERTS_PER_GROUP)
    e_logits = jnp.where(in_group, logits, NEG_INF)
    top1, i1 = first_argmax(e_logits)
    top2, i2 = first_argmax(jnp.where(lane == i1, NEG_INF, e_logits))
    e21 = jnp.exp(top2 - top1)
    gate1 = p_group / (1.0 + e21)
    gate2 = p_group * e21 / (1.0 + e21)

    chosen = ((lane == i1) | (lane == i2)).astype(BF16)
    r = lax.broadcasted_iota(jnp.int32, (tm, tm), 0)
    c = lax.broadcasted_iota(jnp.int32, (tm, tm), 1)
    before = (c < r).astype(BF16)
    rank = jnp.dot(before, chosen, preferred_element_type=F32) + carry_ref[...]
    carry_ref[...] += jnp.sum(chosen.astype(F32), axis=0, keepdims=True)
    cnt_ref[...] = carry_ref[...]
    r1 = jnp.sum(jnp.where(lane == i1, rank, 0.0), axis=-1, keepdims=True).astype(jnp.int32)
    r2 = jnp.sum(jnp.where(lane == i2, rank, 0.0), axis=-1, keepdims=True).astype(jnp.int32)

    idx_ref[...] = jnp.where(lane == 0, i1 - N_GROUPS,
                             jnp.where(lane == 1, i2 - N_GROUPS,
                                       jnp.where(lane == 2, r1, jnp.where(lane == 3, r2, 0))))
    gate_ref[...] = jnp.where(lane == 0, gate1, jnp.where(lane == 1, gate2, 0.0))


def _router(h, g, w_hi, w_lo, bias, tm):
    s, d = h.shape
    row = lambda i: (i, 0)
    fixed = lambda i: (0, 0)
    return pl.pallas_call(
        functools.partial(_router_kernel, tm=tm),
        grid=(s // tm,),
        in_specs=[
            pl.BlockSpec((tm, d), row),
            pl.BlockSpec((1, d), fixed),
            pl.BlockSpec((d, ROUTER_LANES), fixed),
            pl.BlockSpec((d, ROUTER_LANES), fixed),
            pl.BlockSpec((1, ROUTER_LANES), fixed),
        ],
        out_specs=[
            pl.BlockSpec((tm, ROUTER_LANES), row),
            pl.BlockSpec((tm, ROUTER_LANES), row),
            pl.BlockSpec((1, ROUTER_LANES), fixed),
        ],
        out_shape=[
            jax.ShapeDtypeStruct((s, ROUTER_LANES), jnp.int32),
            jax.ShapeDtypeStruct((s, ROUTER_LANES), F32),
            jax.ShapeDtypeStruct((1, ROUTER_LANES), F32),
        ],
        scratch_shapes=[pltpu.VMEM((1, ROUTER_LANES), F32)],
        compiler_params=_params("arbitrary"),
        name="moe_router",
    )(h, g, w_hi, w_lo, bias)


def _row_copy(src_ref, src_row, dst_ref, dst_row, sem):
    return pltpu.make_async_copy(src_ref.at[pl.ds(src_row, 1)], dst_ref.at[pl.ds(dst_row, 1)], sem)


def _pack_bf16_pair(hi, lo):
    hi_bits = lax.bitcast_convert_type(hi.astype(BF16).astype(F32), jnp.uint32)
    lo_bits = lax.bitcast_convert_type(lo.astype(BF16).astype(F32), jnp.uint32)
    return hi_bits | (lo_bits >> 16)


def _unpack_bf16_pair(packed):
    hi = lax.bitcast_convert_type(packed & jnp.uint32(0xFFFF0000), F32)
    lo = lax.bitcast_convert_type(packed << 16, F32)
    return hi, lo


def _wait_rows(src_ref, dst_ref, sem, n):
    pltpu.make_async_copy(src_ref.at[pl.ds(0, n)], dst_ref.at[pl.ds(0, n)], sem).wait()


def _dispatch_kernel(pos_ref, h_ref, g_ref, xs_ref, buf_ref, sem, *, tb, n_steps):
    i = pl.program_id(0)
    s = tb * n_steps
    slot = lax.rem(i, 2)
    half = h_ref.shape[1] // 2

    def wait_slot(sl):
        _wait_rows(buf_ref.at[sl], xs_ref, sem.at[sl], tb)
        _wait_rows(buf_ref.at[sl], xs_ref, sem.at[sl], tb)

    @pl.when(i >= 2)
    def _():
        wait_slot(slot)

    def norm_rows(r, carry):
        rows = pl.ds(pl.multiple_of(r * ROW_CHUNK, ROW_CHUNK), ROW_CHUNK)
        xn = _rms(h_ref[rows, :], g_ref[...])
        buf_ref[slot, rows, :] = _pack_bf16_pair(xn[:, :half], xn[:, half:])
        return carry

    lax.fori_loop(0, tb // ROW_CHUNK, norm_rows, 0, unroll=4)
    base = i * tb

    def issue(t, carry):
        _row_copy(buf_ref.at[slot], t, xs_ref, pos_ref[base + t], sem.at[slot]).start()
        _row_copy(buf_ref.at[slot], t, xs_ref, pos_ref[s + base + t], sem.at[slot]).start()
        return carry

    lax.fori_loop(0, tb, issue, 0, unroll=DMA_ISSUE_UNROLL)

    @pl.when(i == n_steps - 1)
    def _():
        if n_steps > 1:
            wait_slot(1 - slot)
        wait_slot(slot)


def _dispatch(pos, h, g, tb):
    s, d = h.shape
    n_rows = pos.shape[0]
    n_steps = s // tb
    return pl.pallas_call(
        functools.partial(_dispatch_kernel, tb=tb, n_steps=n_steps),
        grid_spec=pltpu.PrefetchScalarGridSpec(
            num_scalar_prefetch=1,
            grid=(n_steps,),
            in_specs=[pl.BlockSpec((tb, d), lambda i, pos: (i, 0)),
                      pl.BlockSpec((1, d), lambda i, pos: (0, 0))],
            out_specs=pl.BlockSpec(memory_space=pl.ANY),
            scratch_shapes=[pltpu.VMEM((2, tb, d // 2), jnp.uint32),
                            pltpu.SemaphoreType.DMA((2,))],
        ),
        out_shape=jax.ShapeDtypeStruct((n_rows, d // 2), jnp.uint32),
        compiler_params=_params("arbitrary"),
        name="moe_dispatch",
    )(pos, h, g)


def _expert_kernel(tile_ref, exp_ref, lo_ref, hi_ref, first_ref, total_ref, x_ref, wg_ref, wu_ref,
                   wd_ref, o_ref):
    p = pl.program_id(0)

    @pl.when(p < total_ref[0])
    def _():
        half = wg_ref.shape[1] // 2
        x_hi, x_lo = _unpack_bf16_pair(x_ref[...])
        x_hi, x_lo = x_hi.astype(BF16), x_lo.astype(BF16)
        a = (jnp.dot(x_hi, wg_ref[0, :half, :], preferred_element_type=F32)
             + jnp.dot(x_lo, wg_ref[0, half:, :], preferred_element_type=F32))
        b = (jnp.dot(x_hi, wu_ref[0, :half, :], preferred_element_type=F32)
             + jnp.dot(x_lo, wu_ref[0, half:, :], preferred_element_type=F32))
        row = lax.broadcasted_iota(jnp.int32, (x_hi.shape[0], 1), 0)
        mine = (row >= lo_ref[p]) & (row < hi_ref[p])
        hid = (a * jax.nn.sigmoid(a) * b).astype(BF16)
        y = jnp.dot(hid, wd_ref[0], preferred_element_type=F32)
        packed = _pack_bf16_pair(y[:, :half], y[:, half:])

        @pl.when(first_ref[p] == 1)
        def _():
            o_ref[...] = packed

        @pl.when(first_ref[p] == 0)
        def _():
            o_ref[...] = jnp.where(mine, packed, o_ref[...])


def _experts(tables, xs, w_gate, w_up, w_down):
    n_rows, half = xs.shape
    d = 2 * half
    tm = EXPERT_TILE
    rows = lambda p, tile, exp, *_: (tile[p], 0)
    wsel = lambda p, tile, exp, *_: (exp[p], 0, 0)
    return pl.pallas_call(
        _expert_kernel,
        grid_spec=pltpu.PrefetchScalarGridSpec(
            num_scalar_prefetch=len(tables),
            grid=(tables[0].shape[0],),
            in_specs=[
                pl.BlockSpec((tm, half), rows),
                pl.BlockSpec((1, d, EXPERT_FF), wsel),
                pl.BlockSpec((1, d, EXPERT_FF), wsel),
                pl.BlockSpec((1, EXPERT_FF, d), wsel),
            ],
            out_specs=pl.BlockSpec((tm, half), rows),
        ),
        out_shape=jax.ShapeDtypeStruct((n_rows, half), jnp.uint32),
        compiler_params=_params("arbitrary"),
        name="moe_experts",
    )(*tables, xs, w_gate, w_up, w_down)


def _combine_kernel(pos_ref, h_ref, gate_ref, g_ref, ys_ref, *refs, tb, n_steps, final):
    if final:
        o_ref, y_ref, sem = refs
    else:
        o_ref, on_ref, y_ref, sem = refs
    i = pl.program_id(0)
    s = tb * n_steps
    slot = lax.rem(i, 2)
    half = h_ref.shape[1] // 2

    def fetch(block, sl):
        base = block * tb

        def issue(t, carry):
            _row_copy(ys_ref, pos_ref[base + t], y_ref.at[sl, 0], t, sem.at[sl]).start()
            _row_copy(ys_ref, pos_ref[s + base + t], y_ref.at[sl, 1], t, sem.at[sl]).start()
            return carry

        lax.fori_loop(0, tb, issue, 0, unroll=DMA_ISSUE_UNROLL)

    @pl.when(i == 0)
    def _():
        fetch(0, 0)

    @pl.when(i + 1 < n_steps)
    def _():
        fetch(i + 1, 1 - slot)

    _wait_rows(ys_ref, y_ref.at[slot, 0], sem.at[slot], tb)
    _wait_rows(ys_ref, y_ref.at[slot, 1], sem.at[slot], tb)
    def combine_rows(r, carry):
        rows = pl.ds(pl.multiple_of(r * ROW_CHUNK, ROW_CHUNK), ROW_CHUNK)
        gates = gate_ref[rows, :]
        g1, g2 = gates[:, 0:1], gates[:, 1:2]
        hi1, lo1 = _unpack_bf16_pair(y_ref[slot, 0, rows, :])
        hi2, lo2 = _unpack_bf16_pair(y_ref[slot, 1, rows, :])
        new_hi = h_ref[rows, :half] + g1 * hi1 + g2 * hi2
        new_lo = h_ref[rows, half:] + g1 * lo1 + g2 * lo2
        sq = (jnp.sum(new_hi * new_hi, axis=-1, keepdims=True)
              + jnp.sum(new_lo * new_lo, axis=-1, keepdims=True))
        inv = lax.rsqrt(sq / (2 * half) + NORM_EPS)
        if final:
            o_ref[rows, :half] = new_hi * inv * g_ref[:, :half]
            o_ref[rows, half:] = new_lo * inv * g_ref[:, half:]
        else:
            o_ref[rows, :half] = new_hi
            o_ref[rows, half:] = new_lo
            on_ref[rows, :half] = (new_hi * inv * g_ref[:, :half]).astype(on_ref.dtype)
            on_ref[rows, half:] = (new_lo * inv * g_ref[:, half:]).astype(on_ref.dtype)
        return carry

    lax.fori_loop(0, tb // ROW_CHUNK, combine_rows, 0, unroll=4)


def _combine(pos, h, gates, g_norm, ys, tb, final):
    s, d = h.shape
    n_steps = s // tb
    row = lambda i, pos: (i, 0)
    out_specs = [pl.BlockSpec((tb, d), row)]
    out_shape = [jax.ShapeDtypeStruct((s, d), F32)]
    if not final:
        out_specs.append(pl.BlockSpec((tb, d), row))
        out_shape.append(jax.ShapeDtypeStruct((s, d), BF16))
    return pl.pallas_call(
        functools.partial(_combine_kernel, tb=tb, n_steps=n_steps, final=final),
        grid_spec=pltpu.PrefetchScalarGridSpec(
            num_scalar_prefetch=1,
            grid=(n_steps,),
            in_specs=[
                pl.BlockSpec((tb, d), row),
                pl.BlockSpec((tb, ROUTER_LANES), row),
                pl.BlockSpec((1, d), lambda i, pos: (0, 0)),
                pl.BlockSpec(memory_space=pl.ANY),
            ],
            out_specs=out_specs,
            scratch_shapes=[pltpu.VMEM((2, 2, tb, d // 2), jnp.uint32),
                            pltpu.SemaphoreType.DMA((2,))],
        ),
        out_shape=out_shape,
        compiler_params=_params("arbitrary"),
        name="moe_combine",
    )(pos, h, gates, g_norm, ys)


def _positions_kernel(idx_ref, start_ref, o_ref):
    idx = idx_ref[...]
    lane = lax.broadcasted_iota(jnp.int32, idx.shape, 1)
    start = start_ref[...]

    def column(c):
        return jnp.sum(jnp.where(lane == c, idx, 0), axis=-1, keepdims=True)

    def seg_start(e):
        return jnp.sum(jnp.where(lane == e, start, 0), axis=-1, keepdims=True)

    pos1 = seg_start(column(0)) + column(2)
    pos2 = seg_start(column(1)) + column(3)
    o_ref[...] = jnp.where(lane == 0, pos1, jnp.where(lane == 1, pos2, 0))


def _positions(idx, seg_start, tb):
    s = idx.shape[0]
    start_row = jnp.pad(seg_start, (0, ROUTER_LANES - N_EXPERTS)).reshape(1, ROUTER_LANES)
    out = pl.pallas_call(
        _positions_kernel,
        grid=(s // tb,),
        in_specs=[pl.BlockSpec((tb, ROUTER_LANES), lambda i: (i, 0)),
                  pl.BlockSpec((1, ROUTER_LANES), lambda i: (0, 0))],
        out_specs=pl.BlockSpec((tb, ROUTER_LANES), lambda i: (i, 0)),
        out_shape=jax.ShapeDtypeStruct((s, ROUTER_LANES), jnp.int32),
        compiler_params=_params("parallel"),
        name="moe_positions",
    )(idx, start_row)
    return jnp.concatenate([out[:, 0], out[:, 1]])


def _count_le(sorted_vals, queries):
    return jnp.sum(sorted_vals[None, :] <= queries[:, None], axis=1).astype(jnp.int32)


def _moe(h, g, g_next, final, w_group, b_group, w_expert, b_expert, w_gate, w_up, w_down):
    s, d = h.shape
    pad = ROUTER_LANES - N_GROUPS - N_EXPERTS
    w_r = jnp.pad(jnp.concatenate([w_group, w_expert], axis=1), ((0, 0), (0, pad)))
    w_hi = w_r.astype(BF16)
    w_lo = (w_r - w_hi.astype(F32)).astype(BF16)
    b_r = jnp.pad(jnp.concatenate([b_group, b_expert]), (0, pad)).reshape(1, ROUTER_LANES)
    idx, gates, counts = _router(h, g, w_hi, w_lo, b_r.astype(F32), tm=min(512, s))

    tm = EXPERT_TILE
    n_tiles = 2 * s // tm
    n_pairs = n_tiles + N_EXPERTS - 1
    counts = counts[0, N_GROUPS:N_GROUPS + N_EXPERTS].astype(jnp.int32)
    seg_end = jnp.cumsum(counts)
    seg_start = seg_end - counts
    pos = _positions(idx, seg_start, tb=min(2048, s))
    tile_row = jnp.arange(n_tiles, dtype=jnp.int32) * tm
    first_e = _count_le(seg_end, tile_row)
    last_e = _count_le(seg_end, tile_row + tm - 1)
    per_tile = last_e - first_e + 1
    pair_end = jnp.cumsum(per_tile)
    pair_start = pair_end - per_tile
    total = pair_end[-1]
    p = jnp.minimum(jnp.arange(n_pairs, dtype=jnp.int32), total - 1)
    p_tile = _count_le(pair_end, p)
    p_expert = first_e[p_tile] + p - pair_start[p_tile]
    p_lo = jnp.clip(seg_start[p_expert] - p_tile * tm, 0, tm)
    p_hi = jnp.clip(seg_end[p_expert] - p_tile * tm, 0, tm)
    p_first = (p_expert == first_e[p_tile]).astype(jnp.int32)

    xs = _dispatch(pos, h, g, tb=min(256, s))
    ys = _experts((p_tile, p_expert, p_lo, p_hi, p_first, total.reshape(1)), xs,
                  w_gate, w_up, w_down)
    return _combine(pos, h, gates, g_next, ys, tb=min(256, s), final=final)


def _rope_tables(s):
    inv_freq = ROPE_THETA ** (-jnp.arange(0, MLA_ROPE_DIM, 2, dtype=F32) / MLA_ROPE_DIM)
    ang = jnp.arange(s, dtype=F32)[:, None] * inv_freq[None, :]
    cos, sin = jnp.cos(ang), jnp.sin(ang)
    zeros = jnp.zeros((s, 128 - MLA_ROPE_DIM), F32)
    return (jnp.concatenate([cos, cos, zeros], axis=1), jnp.concatenate([sin, sin, zeros], axis=1))


def _rot_half_cols(w):
    half = MLA_ROPE_DIM // 2
    return jnp.concatenate([-w[..., half:], w[..., :half]], axis=-1)


def _even_mixer(h, hn, w_in, pool_w, pool_scale, q_norm, w_uq, kv_norm, w_ukv, w_out, ck, sk,
                expert_weights):
    s, d = h.shape
    w_kpe = w_in[:, POOL_DIM + MLA_Q_RANK + MLA_KV_RANK:]
    w_in_ext = jnp.concatenate([w_in, _rot_half_cols(w_kpe)], axis=1).astype(BF16)
    big = min(1024, s)
    z = _matmul([hn], [w_in_ext], BF16, tm=big, tn=896, name="even_in_proj")

    ya = _pool_mixer(z, pool_w.astype(BF16), pool_scale.reshape(1, POOL_DIM), ts=min(512, s))

    w_q = w_uq.reshape(MLA_Q_RANK, MLA_HEADS, MLA_QK_DIM)
    w_q = jnp.concatenate([w_q, _rot_half_cols(w_q[..., MLA_NOPE_DIM:])], axis=-1)
    w_q = jnp.transpose(w_q, (1, 0, 2)).astype(BF16)
    w_kv = w_ukv.reshape(MLA_KV_RANK, MLA_HEADS // 2, 2, MLA_NOPE_DIM + MLA_V_DIM)
    w_k = jnp.transpose(w_kv[..., :MLA_NOPE_DIM], (1, 0, 2, 3))
    w_k = w_k.reshape(MLA_HEADS // 2, MLA_KV_RANK, 2 * MLA_NOPE_DIM).astype(BF16)
    w_vt = jnp.transpose(w_kv[..., MLA_NOPE_DIM:], (1, 2, 3, 0))
    w_vt = w_vt.reshape(MLA_HEADS // 2, 2 * MLA_V_DIM, MLA_KV_RANK).astype(BF16)
    q = _mla_q(z, q_norm.reshape(1, -1), w_q, ck, sk, tm=big, hb=4)
    k, vt = _mla_kv(z, kv_norm.reshape(1, -1), w_k, w_vt, ck, sk, tm=big, hb=4)
    yb, casts = _mla_attention(q, k, vt, tb=min(512, s), n_sub=4 if s >= 2048 else 1,
                               side=expert_weights)

    w_o = w_out.astype(BF16)
    h = _matmul([ya, yb], [w_o[:POOL_DIM], w_o[POOL_DIM:]], F32, tm=big, tn=512,
                residual=h, name="even_out_proj")
    return h, casts


def _odd_mixer(h, hn, w_in, ln_g, ln_b, w_s, b_s, sinks, rel_bias, w_out, expert_weights):
    s, d = h.shape
    big = min(1024, s)
    z, cast_b = _matmul([hn], [w_in.astype(BF16)], BF16, tm=big, tn=512, name="odd_in_proj",
                        side=expert_weights[1:])
    yc = _sgu_mixer(z, ln_g.reshape(1, -1), ln_b.reshape(1, -1), w_s, jnp.transpose(b_s),
                    ts=min(256, s))
    yd, cast_a = _swa_mixer(z, sinks, _swa_bias(rel_bias), side=expert_weights[:1])
    w_o = w_out.astype(BF16)
    h = _matmul([yc, yd], [w_o[:SGU_DIM], w_o[SGU_DIM:]], F32, tm=big, tn=512,
                residual=h, name="odd_out_proj")
    return h, cast_a + cast_b


def kernel(x, mem, norm_mix, norm_xattn, norm_ffn, norm_mem, final_norm, rel_bias, e_w_in, pool_w, pool_scale, mla_q_norm, mla_w_uq, mla_kv_norm, mla_w_ukv, e_w_out, o_w_in, sgu_ln_g, sgu_ln_b, sgu_w, sgu_b, swa_sinks, o_w_out, xa_wq, xa_wk, xa_wv, xa_wo, moe_w_group, moe_b_group, moe_w_expert, moe_b_expert, moe_w_gate, moe_w_up, moe_w_down):
    batch, s, d = x.shape
    assert batch == 1
    depth = norm_mix.shape[0]
    ck, sk = _rope_tables(s)
    h = x.reshape(s, d)
    mem2 = mem.reshape(mem.shape[1], d)
    hn = _rmsnorm(h, norm_mix[0], BF16, tm=min(512, s))
    for layer in range(depth):
        i = layer // 2
        last = layer == depth - 1
        expert_weights = [(moe_w_gate, layer), (moe_w_up, layer), (moe_w_down, layer)]
        if layer % 2 == 0:
            h, casts = _even_mixer(h, hn, e_w_in[i], pool_w[i], pool_scale[i], mla_q_norm[i],
                                   mla_w_uq[i], mla_kv_norm[i], mla_w_ukv[i], e_w_out[i], ck, sk,
                                   expert_weights)
        else:
            h, casts = _odd_mixer(h, hn, o_w_in[i], sgu_ln_g[i], sgu_ln_b[i], sgu_w[i], sgu_b[i],
                                  swa_sinks[i], rel_bias, o_w_out[i], expert_weights)
        mem_n = _rmsnorm(mem2, norm_mem[layer], BF16, tm=mem2.shape[0])
        w_kv_mem = jnp.concatenate([xa_wk[layer], xa_wv[layer]], axis=1).astype(BF16)
        kv_mem = _matmul([mem_n], [w_kv_mem], BF16, tm=mem2.shape[0], tn=2 * XATTN_DIM,
                         name="mem_kv_proj")
        h, more_casts = _xattn(h, norm_xattn[layer].reshape(1, d), xa_wq[layer].astype(BF16),
                               kv_mem, xa_wo[layer].astype(BF16), tm=min(256, s),
                               side=expert_weights[len(casts):])
        w_gate, w_up, w_down = casts + more_casts
        g_next = final_norm if last else norm_mix[layer + 1]
        res = _moe(h, norm_ffn[layer].reshape(1, d), g_next.reshape(1, d), last,
                   moe_w_group[layer], moe_b_group[layer], moe_w_expert[layer],
                   moe_b_expert[layer], w_gate, w_up, w_down)
        if last:
            out, = res
        else:
            h, hn = res
    return out.reshape(batch, s, d)
```

```python
import functools
import math

import jax
import jax.numpy as jnp
from jax import lax
from jax.experimental import pallas as pl
from jax.experimental.pallas import tpu as pltpu

F32 = jnp.float32
BF16 = jnp.bfloat16

NORM_EPS = 1e-6
NEG_INF = -1e30

POOL_WINDOWS = (2, 4, 8, 16)
POOL_GROUP_DIM = 256
POOL_DIM = 1024
POOL_HALO = 16

MLA_HEADS = 24
MLA_Q_RANK = 1024
MLA_KV_RANK = 512
MLA_NOPE_DIM = 128
MLA_ROPE_DIM = 64
MLA_V_DIM = 128
MLA_QK_DIM = MLA_NOPE_DIM + MLA_ROPE_DIM
ROPE_THETA = 10000.0

SGU_GROUPS = 8
SGU_GROUP_DIM = 256
SGU_DIM = 2048
SGU_CHUNK = 128

SWA_HEADS = 32
SWA_KV_HEADS = 8
SWA_HEAD_DIM = 64
SWA_BLOCK = 128
REL_BUCKETS = 32
REL_MAX_DIST = 128

XATTN_HEADS = 4
XATTN_HEAD_DIM = 128
XATTN_DIM = 512

N_GROUPS = 4
EXPERTS_PER_GROUP = 8
N_EXPERTS = 32
EXPERT_FF = 512
ROUTER_LANES = 128
EXPERT_TILE = 256
DMA_ISSUE_UNROLL = 8
ROW_CHUNK = 16
SIDE_CAST_BLOCKS = 128

VMEM_LIMIT_BYTES = 56 * 1024 * 1024


def _params(*semantics):
    return pltpu.CompilerParams(dimension_semantics=semantics, vmem_limit_bytes=VMEM_LIMIT_BYTES)


def _rms(x, g):
    return x * lax.rsqrt(jnp.mean(x * x, axis=-1, keepdims=True) + NORM_EPS) * g


def _rmsnorm_kernel(x_ref, g_ref, o_ref):
    o_ref[...] = _rms(x_ref[...].astype(F32), g_ref[...]).astype(o_ref.dtype)


def _rmsnorm(x, g, out_dtype, tm):
    m, d = x.shape
    return pl.pallas_call(
        _rmsnorm_kernel,
        grid=(m // tm,),
        in_specs=[pl.BlockSpec((tm, d), lambda i: (i, 0)), pl.BlockSpec((1, d), lambda i: (0, 0))],
        out_specs=pl.BlockSpec((tm, d), lambda i: (i, 0)),
        out_shape=jax.ShapeDtypeStruct((m, d), out_dtype),
        compiler_params=_params("parallel"),
        name="rmsnorm",
    )(x, g.reshape(1, d).astype(F32))


def _side_cast_plan(w, layer, host_steps, step_of):
    depth, e, a, b = w.shape
    n_blocks = SIDE_CAST_BLOCKS
    while n_blocks > host_steps:
        n_blocks //= 2
    rows = e * a // n_blocks

    def block_of(*idx):
        return jnp.minimum(step_of(*idx), n_blocks - 1)

    return dict(
        operand=w.reshape(depth, e * a, b),
        in_spec=pl.BlockSpec((None, rows, b), lambda *idx: (layer, block_of(*idx), 0)),
        out_spec=pl.BlockSpec((rows, b), lambda *idx: (block_of(*idx), 0)),
        out_shape=jax.ShapeDtypeStruct((e * a, b), BF16),
        shape=(e, a, b),
    )


def _host_call(body, side, operands, *, grid, in_specs, out_spec, out_shape, semantics, name,
               step_of, scratch_shapes=()):
    plans = [_side_cast_plan(w, layer, math.prod(grid), step_of) for w, layer in side]
    single = not isinstance(out_spec, (list, tuple))
    out_specs = [out_spec] if single else list(out_spec)
    out_shapes = [out_shape] if single else list(out_shape)
    n_in, n_side, n_out = len(in_specs), len(plans), len(out_specs)
    first_out = n_in + n_side
    first_side_out = first_out + n_out

    def wrapped(*refs):
        side_in = refs[n_in:first_out]
        side_out = refs[first_side_out:first_side_out + n_side]
        for src, dst in zip(side_in, side_out):
            dst[...] = src[...].astype(dst.dtype)
        body(*refs[:n_in], *refs[first_out:first_side_out], *refs[first_side_out + n_side:])

    res = pl.pallas_call(
        wrapped,
        grid=grid,
        in_specs=list(in_specs) + [p["in_spec"] for p in plans],
        out_specs=out_specs + [p["out_spec"] for p in plans],
        out_shape=out_shapes + [p["out_shape"] for p in plans],
        scratch_shapes=list(scratch_shapes),
        compiler_params=_params(*semantics),
        name=name,
    )(*operands, *[p["operand"] for p in plans])
    casts = [r.reshape(p["shape"]) for r, p in zip(res[n_out:], plans)]
    return (res[0] if single else list(res[:n_out])), casts


def _matmul_kernel(*refs, n_in, has_res):
    o_ref = refs[-1]
    acc = jnp.dot(refs[0][...], refs[n_in][...], preferred_element_type=F32)
    for k in range(1, n_in):
        acc += jnp.dot(refs[k][...], refs[n_in + k][...], preferred_element_type=F32)
    if has_res:
        acc += refs[2 * n_in][...]
    o_ref[...] = acc.astype(o_ref.dtype)


def _matmul(xs, ws, out_dtype, tm, tn, residual=None, name="matmul", side=()):
    m = xs[0].shape[0]
    n = ws[0].shape[1]
    in_specs = [pl.BlockSpec((tm, x.shape[1]), lambda i, j: (i, 0)) for x in xs]
    in_specs += [pl.BlockSpec((w.shape[0], tn), lambda i, j: (0, j)) for w in ws]
    args = list(xs) + list(ws)
    if residual is not None:
        in_specs.append(pl.BlockSpec((tm, tn), lambda i, j: (i, j)))
        args.append(residual)
    n_j = n // tn
    out, casts = _host_call(
        functools.partial(_matmul_kernel, n_in=len(xs), has_res=residual is not None),
        side, args,
        grid=(m // tm, n_j),
        in_specs=in_specs,
        out_spec=pl.BlockSpec((tm, tn), lambda i, j: (i, j)),
        out_shape=jax.ShapeDtypeStruct((m, n), out_dtype),
        semantics=("arbitrary", "arbitrary") if side else ("parallel", "parallel"),
        name=name,
        step_of=lambda i, j: i * n_j + j,
    )
    return (out, casts) if side else out


def _pool_kernel(a_ref, halo_ref, w_ref, scale_ref, o_ref, *, ts):
    i = pl.program_id(0)
    g = pl.program_id(1)
    win = jnp.left_shift(2, g)
    a = a_ref[...]
    halo = jnp.where(i > 0, halo_ref[...], jnp.zeros_like(halo_ref))
    ext = jnp.concatenate([halo, a], axis=0)
    row = lax.broadcasted_iota(jnp.int32, (ts, POOL_HALO + ts), 0) + POOL_HALO
    col = lax.broadcasted_iota(jnp.int32, (ts, POOL_HALO + ts), 1)
    band = ((col <= row) & (col > row - win)).astype(BF16)
    wsum = jnp.dot(band, ext, preferred_element_type=F32)
    t = i * ts + lax.broadcasted_iota(jnp.int32, (ts, 1), 0)
    count = jnp.minimum(t + 1, win).astype(F32)
    d = (wsum / count - a.astype(F32)).astype(BF16)
    y = jnp.dot(d, w_ref[0], preferred_element_type=F32) * scale_ref[...]
    o_ref[...] = y.astype(o_ref.dtype)


def _pool_mixer(z, pool_w, pool_scale, ts):
    s = z.shape[0]
    c = POOL_GROUP_DIM
    hb = ts // POOL_HALO
    return pl.pallas_call(
        functools.partial(_pool_kernel, ts=ts),
        grid=(s // ts, len(POOL_WINDOWS)),
        in_specs=[
            pl.BlockSpec((ts, c), lambda i, g: (i, g)),
            pl.BlockSpec((POOL_HALO, c), lambda i, g: (jnp.maximum(i * hb - 1, 0), g)),
            pl.BlockSpec((1, c, c), lambda i, g: (g, 0, 0)),
            pl.BlockSpec((1, c), lambda i, g: (0, g)),
        ],
        out_specs=pl.BlockSpec((ts, c), lambda i, g: (i, g)),
        out_shape=jax.ShapeDtypeStruct((s, POOL_DIM), BF16),
        compiler_params=_params("parallel", "parallel"),
        name="pool_mixer",
    )(z, z, pool_w, pool_scale)


def _rope_half_block(blk, ck, sk):
    return blk * ck + pltpu.roll(blk, 64, 1) * sk


def _mla_q_kernel(cq_ref, g_ref, w_ref, ck_ref, sk_ref, o_ref, xn_ref, *, hb, scale):
    @pl.when(pl.program_id(1) == 0)
    def _():
        xn_ref[...] = _rms(cq_ref[...].astype(F32), g_ref[...]).astype(BF16)

    xn = xn_ref[...]
    for hh in range(hb):
        q = jnp.dot(xn, w_ref[hh], preferred_element_type=F32)
        o_ref[hh, :, 0:MLA_NOPE_DIM] = (q[:, :MLA_NOPE_DIM] * scale).astype(o_ref.dtype)
        pe = _rope_half_block(q[:, MLA_NOPE_DIM:], ck_ref[...], sk_ref[...]) * scale
        o_ref[hh, :, MLA_NOPE_DIM:MLA_QK_DIM] = pe[:, :MLA_ROPE_DIM].astype(o_ref.dtype)


def _mla_q(z, q_norm, w_q, ck, sk, tm, hb):
    s = z.shape[0]
    scale = MLA_QK_DIM ** -0.5 * math.log2(math.e)
    return pl.pallas_call(
        functools.partial(_mla_q_kernel, hb=hb, scale=scale),
        grid=(s // tm, MLA_HEADS // hb),
        in_specs=[
            pl.BlockSpec((tm, MLA_Q_RANK), lambda i, j: (i, POOL_DIM // MLA_Q_RANK)),
            pl.BlockSpec((1, MLA_Q_RANK), lambda i, j: (0, 0)),
            pl.BlockSpec((hb, MLA_Q_RANK, 256), lambda i, j: (j, 0, 0)),
            pl.BlockSpec((tm, 128), lambda i, j: (i, 0)),
            pl.BlockSpec((tm, 128), lambda i, j: (i, 0)),
        ],
        out_specs=pl.BlockSpec((hb, tm, MLA_QK_DIM), lambda i, j: (j, i, 0)),
        out_shape=jax.ShapeDtypeStruct((MLA_HEADS, s, MLA_QK_DIM), BF16),
        scratch_shapes=[pltpu.VMEM((tm, MLA_Q_RANK), BF16)],
        compiler_params=_params("parallel", "arbitrary"),
        name="mla_q_proj",
    )(z, q_norm, w_q, ck, sk)


def _mla_kv_kernel(ckv_ref, kpe_ref, g_ref, wk_ref, wvt_ref, ck_ref, sk_ref, k_ref, vt_ref,
                   xn_ref, kr_ref, *, hb):
    @pl.when(pl.program_id(1) == 0)
    def _():
        xn_ref[...] = _rms(ckv_ref[...].astype(F32), g_ref[...]).astype(BF16)
        kr = _rope_half_block(kpe_ref[...].astype(F32), ck_ref[...], sk_ref[...])
        kr_ref[...] = kr.astype(BF16)

    xn = xn_ref[...]
    for pair in range(hb // 2):
        kk = jnp.dot(xn, wk_ref[pair], preferred_element_type=F32)
        vv = lax.dot_general(wvt_ref[pair], xn, (((1,), (1,)), ((), ())),
                             preferred_element_type=F32)
        for side in range(2):
            hh = 2 * pair + side
            k_ref[hh, :, 0:MLA_NOPE_DIM] = (
                kk[:, side * MLA_NOPE_DIM:(side + 1) * MLA_NOPE_DIM].astype(k_ref.dtype))
            k_ref[hh, :, MLA_NOPE_DIM:MLA_QK_DIM] = kr_ref[:, 0:MLA_ROPE_DIM]
            vt_ref[hh] = vv[side * MLA_V_DIM:(side + 1) * MLA_V_DIM, :].astype(vt_ref.dtype)


def _mla_kv(z, kv_norm, w_k, w_vt, ck, sk, tm, hb):
    s = z.shape[0]
    ckv_block = (POOL_DIM + MLA_Q_RANK) // MLA_KV_RANK
    kpe_block = (POOL_DIM + MLA_Q_RANK + MLA_KV_RANK) // 128
    return pl.pallas_call(
        functools.partial(_mla_kv_kernel, hb=hb),
        grid=(s // tm, MLA_HEADS // hb),
        in_specs=[
            pl.BlockSpec((tm, MLA_KV_RANK), lambda i, j: (i, ckv_block)),
            pl.BlockSpec((tm, 128), lambda i, j: (i, kpe_block)),
            pl.BlockSpec((1, MLA_KV_RANK), lambda i, j: (0, 0)),
            pl.BlockSpec((hb // 2, MLA_KV_RANK, 2 * MLA_NOPE_DIM), lambda i, j: (j, 0, 0)),
            pl.BlockSpec((hb // 2, 2 * MLA_V_DIM, MLA_KV_RANK), lambda i, j: (j, 0, 0)),
            pl.BlockSpec((tm, 128), lambda i, j: (i, 0)),
            pl.BlockSpec((tm, 128), lambda i, j: (i, 0)),
        ],
        out_specs=[
            pl.BlockSpec((hb, tm, MLA_QK_DIM), lambda i, j: (j, i, 0)),
            pl.BlockSpec((hb, MLA_V_DIM, tm), lambda i, j: (j, 0, i)),
        ],
        out_shape=[
            jax.ShapeDtypeStruct((MLA_HEADS, s, MLA_QK_DIM), BF16),
            jax.ShapeDtypeStruct((MLA_HEADS, MLA_V_DIM, s), BF16),
        ],
        scratch_shapes=[pltpu.VMEM((tm, MLA_KV_RANK), BF16), pltpu.VMEM((tm, 128), BF16)],
        compiler_params=_params("parallel", "arbitrary"),
        name="mla_kv_proj",
    )(z, z, kv_norm, w_k, w_vt, ck, sk)


def _mla_attn_kernel(q_ref, k_ref, v_ref, o_ref, m_ref, l_ref, acc_ref, s_ref, *, tb, n_sub):
    i = pl.program_id(1)
    m_ref[...] = jnp.full_like(m_ref, NEG_INF)
    l_ref[...] = jnp.zeros_like(l_ref)
    acc_ref[...] = jnp.zeros_like(acc_ref)

    def scores(x, kv):
        start = pl.multiple_of(kv * tb, tb)
        q = q_ref[0, x * tb:(x + 1) * tb, :]
        k = k_ref[0, pl.ds(start, tb), :]
        return lax.dot_general(k, q, (((1,), (1,)), ((), ())), preferred_element_type=F32)

    def update(x, kv, s, masked):
        start = pl.multiple_of(kv * tb, tb)
        vt = v_ref[0, :, pl.ds(start, tb)]
        if masked:
            key = lax.broadcasted_iota(jnp.int32, (tb, tb), 0)
            qry = lax.broadcasted_iota(jnp.int32, (tb, tb), 1)
            s = jnp.where(key <= qry, s, NEG_INF)
        m_prev = m_ref[x]
        m_new = jnp.maximum(m_prev, jnp.max(s, axis=0, keepdims=True))
        alpha = jnp.exp2(m_prev - m_new)
        p = jnp.exp2(s - m_new)
        l_ref[x] = alpha * l_ref[x] + jnp.sum(p, axis=0, keepdims=True)
        acc_ref[x] = alpha * acc_ref[x] + jnp.dot(vt, p.astype(BF16), preferred_element_type=F32)
        m_ref[x] = m_new

    for x in range(n_sub):
        s_ref[x] = scores(x, 0)

    def body(kv, carry):
        for x in range(n_sub):
            s = s_ref[x]
            s_ref[x] = scores(x, kv + 1)
            update(x, kv, s, False)
        return carry

    first_diag = n_sub * i
    lax.fori_loop(0, first_diag, body, 0)
    for d in range(n_sub):
        for x in range(d, n_sub):
            s = s_ref[x] if d == 0 else scores(x, first_diag + d)
            update(x, first_diag + d, s, x == d)
    for x in range(n_sub):
        o_ref[x * tb:(x + 1) * tb, :] = jnp.transpose(acc_ref[x] / l_ref[x]).astype(o_ref.dtype)


def _mla_attention(q, k, v, tb, n_sub, side=()):
    s = q.shape[1]
    tq = tb * n_sub
    n_q = s // tq
    return _host_call(
        functools.partial(_mla_attn_kernel, tb=tb, n_sub=n_sub),
        side, (q, k, v),
        grid=(MLA_HEADS, n_q),
        in_specs=[
            pl.BlockSpec((1, tq, MLA_QK_DIM), lambda h, i: (h, i, 0)),
            pl.BlockSpec((1, s, MLA_QK_DIM), lambda h, i: (h, 0, 0)),
            pl.BlockSpec((1, MLA_V_DIM, s), lambda h, i: (h, 0, 0)),
        ],
        out_spec=pl.BlockSpec((tq, MLA_V_DIM), lambda h, i: (i, h)),
        out_shape=jax.ShapeDtypeStruct((s, MLA_HEADS * MLA_V_DIM), BF16),
        scratch_shapes=[pltpu.VMEM((n_sub, 1, tb), F32), pltpu.VMEM((n_sub, 1, tb), F32),
                        pltpu.VMEM((n_sub, MLA_V_DIM, tb), F32),
                        pltpu.VMEM((n_sub, tb, tb), F32)],
        semantics=("arbitrary", "arbitrary"),
        name="mla_attention",
        step_of=lambda h, i: h * n_q + i,
    )


def _gelu_tanh(x):
    return 0.5 * x * (1.0 + jnp.tanh(math.sqrt(2.0 / math.pi) * (x + 0.044715 * (x * x * x))))


def _sgu_kernel(u_ref, v_ref, g_ref, b_ref, w_ref, bs_ref, o_ref, *, ts):
    row = lax.broadcasted_iota(jnp.int32, (SGU_CHUNK, SGU_CHUNK), 0)
    col = lax.broadcasted_iota(jnp.int32, (SGU_CHUNK, SGU_CHUNK), 1)
    causal = col <= row
    for c in range(ts // SGU_CHUNK):
        rows = slice(c * SGU_CHUNK, (c + 1) * SGU_CHUNK)
        v = _gelu_tanh(v_ref[rows, :].astype(F32))
        mu = jnp.mean(v, axis=-1, keepdims=True)
        vc = v - mu
        vn = vc * lax.rsqrt(jnp.mean(vc * vc, axis=-1, keepdims=True) + NORM_EPS)
        vn = (vn * g_ref[...] + b_ref[...]).astype(BF16)
        for g in range(SGU_GROUPS):
            cols = slice(g * SGU_GROUP_DIM, (g + 1) * SGU_GROUP_DIM)
            w = jnp.where(causal, w_ref[g], 0.0).astype(BF16)
            mixed = jnp.dot(w, vn[:, cols], preferred_element_type=F32) + bs_ref[:, g:g + 1]
            u = _gelu_tanh(u_ref[rows, cols].astype(F32))
            o_ref[rows, cols] = (u * mixed).astype(o_ref.dtype)


def _sgu_mixer(z, ln_g, ln_b, w_s, b_s_t, ts):
    s = z.shape[0]
    return pl.pallas_call(
        functools.partial(_sgu_kernel, ts=ts),
        grid=(s // ts,),
        in_specs=[
            pl.BlockSpec((ts, SGU_DIM), lambda i: (i, 0)),
            pl.BlockSpec((ts, SGU_DIM), lambda i: (i, 1)),
            pl.BlockSpec((1, SGU_DIM), lambda i: (0, 0)),
            pl.BlockSpec((1, SGU_DIM), lambda i: (0, 0)),
            pl.BlockSpec((SGU_GROUPS, SGU_CHUNK, SGU_CHUNK), lambda i: (0, 0, 0)),
            pl.BlockSpec((SGU_CHUNK, SGU_GROUPS), lambda i: (0, 0)),
        ],
        out_specs=pl.BlockSpec((ts, SGU_DIM), lambda i: (i, 0)),
        out_shape=jax.ShapeDtypeStruct((s, SGU_DIM), BF16),
        compiler_params=_params("parallel"),
        name="sgu_mixer",
    )(z, z, ln_g, ln_b, w_s, b_s_t)


def _swa_kernel(sink_ref, q_ref, kp_ref, kc_ref, vp_ref, vc_ref, bias_ref, o_ref):
    n = pl.program_id(0)
    blk = SWA_BLOCK
    hd = SWA_HEAD_DIM
    col = lax.broadcasted_iota(jnp.int32, (blk, 2 * blk), 1)
    no_prev = (n == 0) & (col < blk)
    q_low = lax.broadcasted_iota(jnp.int32, (blk, 2 * hd), 1) < hd
    kv_low = lax.broadcasted_iota(jnp.int32, (2 * blk, 2 * hd), 1) < hd
    scale = hd ** -0.5
    for pair in range(SWA_KV_HEADS // 2):
        cols = slice(pair * 2 * hd, (pair + 1) * 2 * hd)
        kp = jnp.concatenate([kp_ref[:, cols], kc_ref[:, cols]], axis=0).astype(F32)
        vp = jnp.concatenate([vp_ref[:, cols], vc_ref[:, cols]], axis=0).astype(F32)
        k_swapped = pltpu.roll(kp, hd, 1)
        v_swapped = pltpu.roll(vp, hd, 1)
        for half in range(2):
            kvh = 2 * pair + half
            own = kv_low if half == 0 else ~kv_low
            k_dup = jnp.where(own, kp, k_swapped).astype(BF16)
            v_dup = jnp.where(own, vp, v_swapped)
            v_side = (jnp.where(kv_low, v_dup, 0.0).astype(BF16),
                      jnp.where(kv_low, 0.0, v_dup).astype(BF16))
            for t in range(2):
                tile = 2 * kvh + t
                qt = q_ref[:, tile * 2 * hd:(tile + 1) * 2 * hd].astype(F32) * scale
                o = None
                for side in range(2):
                    h = 2 * tile + side
                    qh = jnp.where(q_low if side == 0 else ~q_low, qt, 0.0).astype(BF16)
                    s = lax.dot_general(qh, k_dup, (((1,), (1,)), ((), ())),
                                        preferred_element_type=F32)
                    s = jnp.where(no_prev, NEG_INF, s + bias_ref[h])
                    sink = sink_ref[h]
                    m = jnp.maximum(jnp.max(s, axis=-1, keepdims=True), sink)
                    p = jnp.exp(s - m)
                    denom = jnp.sum(p, axis=-1, keepdims=True) + jnp.exp(sink - m)
                    part = jnp.dot(p.astype(BF16), v_side[side],
                                   preferred_element_type=F32) / denom
                    o = part if o is None else o + part
                o_ref[:, tile * 2 * hd:(tile + 1) * 2 * hd] = o.astype(o_ref.dtype)


def _swa_mixer(z, sinks, bias, side=()):
    s = z.shape[0]
    blk = SWA_BLOCK
    qw = SWA_HEADS * SWA_HEAD_DIM
    kw = SWA_KV_HEADS * SWA_HEAD_DIM
    q_block = 2 * SGU_DIM // qw
    k_block = (2 * SGU_DIM + qw) // kw
    v_block = k_block + 1
    prev = lambda n: jnp.maximum(n - 1, 0)
    return _host_call(
        _swa_kernel,
        side, (sinks, z, z, z, z, z, bias),
        grid=(s // blk,),
        in_specs=[
            pl.BlockSpec(memory_space=pltpu.SMEM),
            pl.BlockSpec((blk, qw), lambda n: (n, q_block)),
            pl.BlockSpec((blk, kw), lambda n: (prev(n), k_block)),
            pl.BlockSpec((blk, kw), lambda n: (n, k_block)),
            pl.BlockSpec((blk, kw), lambda n: (prev(n), v_block)),
            pl.BlockSpec((blk, kw), lambda n: (n, v_block)),
            pl.BlockSpec((SWA_HEADS, blk, 2 * blk), lambda n: (0, 0, 0)),
        ],
        out_spec=pl.BlockSpec((blk, qw), lambda n: (n, 0)),
        out_shape=jax.ShapeDtypeStruct((s, qw), BF16),
        semantics=("arbitrary",),
        name="swa_mixer",
        step_of=lambda n: n,
    )


def _t5_bucket(dist):
    max_exact = REL_BUCKETS // 2
    large = max_exact + (jnp.log(jnp.maximum(dist, 1).astype(F32) / max_exact)
                         / math.log(REL_MAX_DIST / max_exact) * (REL_BUCKETS - max_exact)).astype(jnp.int32)
    large = jnp.minimum(large, REL_BUCKETS - 1)
    return jnp.where(dist < max_exact, dist, large)


def _swa_bias(rel_bias):
    blk = SWA_BLOCK
    by_dist = rel_bias[_t5_bucket(jnp.arange(blk))].astype(F32)
    outside = jnp.full_like(by_dist, NEG_INF)
    diag = jnp.concatenate([outside, by_dist[::-1], outside], axis=0)
    period = 3 * blk
    diag = jnp.roll(diag, -(blk - 1), axis=0)
    skew = jnp.tile(diag, (blk, 1))[:blk * (period - 1)].reshape(blk, period - 1, SWA_HEADS)
    return jnp.transpose(skew[:, :2 * blk], (2, 0, 1))


def _xattn_kernel(h_ref, g_ref, wq_ref, kv_ref, wo_ref, gr_ref, whi_ref, wlo_ref, br_ref,
                  o_ref, idx_ref, gate_ref, cnt_ref, carry_ref):
    x = h_ref[...]
    xn = _rms(x, g_ref[...]).astype(BF16)
    q = jnp.dot(xn, wq_ref[...], preferred_element_type=F32) * (XATTN_HEAD_DIM ** -0.5)
    q = q.astype(BF16)
    outs = []
    for hd in range(XATTN_HEADS):
        cols = slice(hd * XATTN_HEAD_DIM, (hd + 1) * XATTN_HEAD_DIM)
        k = kv_ref[:, cols]
        v = kv_ref[:, XATTN_DIM + hd * XATTN_HEAD_DIM:XATTN_DIM + (hd + 1) * XATTN_HEAD_DIM]
        s = lax.dot_general(q[:, cols], k, (((1,), (1,)), ((), ())), preferred_element_type=F32)
        p = jnp.exp(s - jnp.max(s, axis=-1, keepdims=True))
        denom = jnp.sum(p, axis=-1, keepdims=True)
        o = jnp.dot(p.astype(BF16), v, preferred_element_type=F32) / denom
        outs.append(o.astype(BF16))
    o = jnp.concatenate(outs, axis=-1)
    h_new = x + jnp.dot(o, wo_ref[...], preferred_element_type=F32)
    o_ref[...] = h_new
    _route(h_new, gr_ref, whi_ref, wlo_ref, br_ref, idx_ref, gate_ref, cnt_ref, carry_ref)


def _xattn_and_route(h, g, wq, kv_mem, wo, g_route, router_weights, tm, side=()):
    s, d = h.shape
    mlen = kv_mem.shape[0]
    row = lambda i: (i, 0)
    fixed = lambda i: (0, 0)
    return _host_call(
        _xattn_kernel,
        side, (h, g, wq, kv_mem, wo, g_route, *router_weights),
        grid=(s // tm,),
        in_specs=[
            pl.BlockSpec((tm, d), row),
            pl.BlockSpec((1, d), fixed),
            pl.BlockSpec((d, XATTN_DIM), fixed),
            pl.BlockSpec((mlen, 2 * XATTN_DIM), fixed),
            pl.BlockSpec((XATTN_DIM, d), fixed),
            pl.BlockSpec((1, d), fixed),
            pl.BlockSpec((d, ROUTER_LANES), fixed),
            pl.BlockSpec((d, ROUTER_LANES), fixed),
            pl.BlockSpec((1, ROUTER_LANES), fixed),
        ],
        out_spec=[
            pl.BlockSpec((tm, d), row),
            pl.BlockSpec((tm, ROUTER_LANES), row),
            pl.BlockSpec((tm, ROUTER_LANES), row),
            pl.BlockSpec((1, ROUTER_LANES), fixed),
        ],
        out_shape=[
            jax.ShapeDtypeStruct((s, d), F32),
            jax.ShapeDtypeStruct((s, ROUTER_LANES), jnp.int32),
            jax.ShapeDtypeStruct((s, ROUTER_LANES), F32),
            jax.ShapeDtypeStruct((1, ROUTER_LANES), F32),
        ],
        scratch_shapes=[pltpu.VMEM((1, ROUTER_LANES), F32)],
        semantics=("arbitrary",),
        name="memory_xattn_route",
        step_of=lambda i: i,
    )


def _route(h, g_ref, whi_ref, wlo_ref, b_ref, idx_ref, gate_ref, cnt_ref, carry_ref):
    tm = h.shape[0]

    @pl.when(pl.program_id(0) == 0)
    def _():
        carry_ref[...] = jnp.zeros_like(carry_ref)

    xn = _rms(h, g_ref[...])
    x_hi = xn.astype(BF16)
    x_lo = (xn - x_hi.astype(F32)).astype(BF16)
    logits = (jnp.dot(x_hi, whi_ref[...], preferred_element_type=F32)
              + jnp.dot(x_lo, whi_ref[...], preferred_element_type=F32)
              + jnp.dot(x_hi, wlo_ref[...], preferred_element_type=F32)) + b_ref[...]

    lane = lax.broadcasted_iota(jnp.int32, (tm, ROUTER_LANES), 1)
    big = jnp.int32(ROUTER_LANES)

    def first_argmax(vals):
        top = jnp.max(vals, axis=-1, keepdims=True)
        idx = jnp.min(jnp.where(vals == top, lane, big), axis=-1, keepdims=True)
        return top, idx

    is_group = lane < N_GROUPS
    g_logits = jnp.where(is_group, logits, NEG_INF)
    g_top, g_idx = first_argmax(g_logits)
    p_group = 1.0 / jnp.sum(jnp.where(is_group, jnp.exp(g_logits - g_top), 0.0), axis=-1,
                            keepdims=True)
    lo = N_GROUPS + g_idx * EXPERTS_PER_GROUP
    in_group = (lane >= lo) & (lane < lo + EXPERTS_PER_GROUP)
    e_logits = jnp.where(in_group, logits, NEG_INF)
    top1, i1 = first_argmax(e_logits)
    top2, i2 = first_argmax(jnp.where(lane == i1, NEG_INF, e_logits))
    e21 = jnp.exp(top2 - top1)
    gate1 = p_group / (1.0 + e21)
    gate2 = p_group * e21 / (1.0 + e21)

    chosen = ((lane == i1) | (lane == i2)).astype(BF16)
    r = lax.broadcasted_iota(jnp.int32, (tm, tm), 0)
    c = lax.broadcasted_iota(jnp.int32, (tm, tm), 1)
    before = (c < r).astype(BF16)
    rank = jnp.dot(before, chosen, preferred_element_type=F32) + carry_ref[...]
    carry_ref[...] += jnp.sum(chosen.astype(F32), axis=0, keepdims=True)
    cnt_ref[...] = carry_ref[...]
    r1 = jnp.sum(jnp.where(lane == i1, rank, 0.0), axis=-1, keepdims=True).astype(jnp.int32)
    r2 = jnp.sum(jnp.where(lane == i2, rank, 0.0), axis=-1, keepdims=True).astype(jnp.int32)

    idx_ref[...] = jnp.where(lane == 0, i1 - N_GROUPS,
                             jnp.where(lane == 1, i2 - N_GROUPS,
                                       jnp.where(lane == 2, r1, jnp.where(lane == 3, r2, 0))))
    gate_ref[...] = jnp.where(lane == 0, gate1, jnp.where(lane == 1, gate2, 0.0))


def _router_weights(w_group, b_group, w_expert, b_expert):
    pad = ROUTER_LANES - N_GROUPS - N_EXPERTS
    w_r = jnp.pad(jnp.concatenate([w_group, w_expert], axis=1), ((0, 0), (0, pad)))
    w_hi = w_r.astype(BF16)
    w_lo = (w_r - w_hi.astype(F32)).astype(BF16)
    b_r = jnp.pad(jnp.concatenate([b_group, b_expert]), (0, pad)).reshape(1, ROUTER_LANES)
    return w_hi, w_lo, b_r.astype(F32)


def _row_copy(src_ref, src_row, dst_ref, dst_row, sem):
    return pltpu.make_async_copy(src_ref.at[pl.ds(src_row, 1)], dst_ref.at[pl.ds(dst_row, 1)], sem)


def _pack_bf16_pair(hi, lo):
    hi_bits = lax.bitcast_convert_type(hi.astype(BF16).astype(F32), jnp.uint32)
    lo_bits = lax.bitcast_convert_type(lo.astype(BF16).astype(F32), jnp.uint32)
    return hi_bits | (lo_bits >> 16)


def _unpack_bf16_pair(packed):
    hi = lax.bitcast_convert_type(packed & jnp.uint32(0xFFFF0000), F32)
    lo = lax.bitcast_convert_type(packed << 16, F32)
    return hi, lo


def _wait_rows(src_ref, dst_ref, sem, n):
    pltpu.make_async_copy(src_ref.at[pl.ds(0, n)], dst_ref.at[pl.ds(0, n)], sem).wait()


def _dispatch_kernel(pos_ref, h_ref, g_ref, xs_ref, buf_ref, sem, *, tb, n_steps):
    i = pl.program_id(0)
    s = tb * n_steps
    slot = lax.rem(i, 2)
    half = h_ref.shape[1] // 2

    def wait_slot(sl):
        _wait_rows(buf_ref.at[sl], xs_ref, sem.at[sl], tb)
        _wait_rows(buf_ref.at[sl], xs_ref, sem.at[sl], tb)

    @pl.when(i >= 2)
    def _():
        wait_slot(slot)

    def norm_rows(r, carry):
        rows = pl.ds(pl.multiple_of(r * ROW_CHUNK, ROW_CHUNK), ROW_CHUNK)
        xn = _rms(h_ref[rows, :], g_ref[...])
        buf_ref[slot, rows, :] = _pack_bf16_pair(xn[:, :half], xn[:, half:])
        return carry

    lax.fori_loop(0, tb // ROW_CHUNK, norm_rows, 0, unroll=4)
    base = i * tb

    def issue(t, carry):
        _row_copy(buf_ref.at[slot], t, xs_ref, pos_ref[base + t], sem.at[slot]).start()
        _row_copy(buf_ref.at[slot], t, xs_ref, pos_ref[s + base + t], sem.at[slot]).start()
        return carry

    lax.fori_loop(0, tb, issue, 0, unroll=DMA_ISSUE_UNROLL)

    @pl.when(i == n_steps - 1)
    def _():
        if n_steps > 1:
            wait_slot(1 - slot)
        wait_slot(slot)


def _dispatch(pos, h, g, tb):
    s, d = h.shape
    n_rows = pos.shape[0]
    n_steps = s // tb
    return pl.pallas_call(
        functools.partial(_dispatch_kernel, tb=tb, n_steps=n_steps),
        grid_spec=pltpu.PrefetchScalarGridSpec(
            num_scalar_prefetch=1,
            grid=(n_steps,),
            in_specs=[pl.BlockSpec((tb, d), lambda i, pos: (i, 0)),
                      pl.BlockSpec((1, d), lambda i, pos: (0, 0))],
            out_specs=pl.BlockSpec(memory_space=pl.ANY),
            scratch_shapes=[pltpu.VMEM((2, tb, d // 2), jnp.uint32),
                            pltpu.SemaphoreType.DMA((2,))],
        ),
        out_shape=jax.ShapeDtypeStruct((n_rows, d // 2), jnp.uint32),
        compiler_params=_params("arbitrary"),
        name="moe_dispatch",
    )(pos, h, g)


def _expert_kernel(tile_ref, exp_ref, lo_ref, hi_ref, first_ref, total_ref, x_ref, wg_ref, wu_ref,
                   wd_ref, o_ref):
    p = pl.program_id(0)

    @pl.when(p < total_ref[0])
    def _():
        half = wg_ref.shape[1] // 2
        x_hi, x_lo = _unpack_bf16_pair(x_ref[...])
        x_hi, x_lo = x_hi.astype(BF16), x_lo.astype(BF16)
        a = (jnp.dot(x_hi, wg_ref[0, :half, :], preferred_element_type=F32)
             + jnp.dot(x_lo, wg_ref[0, half:, :], preferred_element_type=F32))
        b = (jnp.dot(x_hi, wu_ref[0, :half, :], preferred_element_type=F32)
             + jnp.dot(x_lo, wu_ref[0, half:, :], preferred_element_type=F32))
        row = lax.broadcasted_iota(jnp.int32, (x_hi.shape[0], 1), 0)
        mine = (row >= lo_ref[p]) & (row < hi_ref[p])
        hid = (a * jax.nn.sigmoid(a) * b).astype(BF16)
        y = jnp.dot(hid, wd_ref[0], preferred_element_type=F32)
        packed = _pack_bf16_pair(y[:, :half], y[:, half:])

        @pl.when(first_ref[p] == 1)
        def _():
            o_ref[...] = packed

        @pl.when(first_ref[p] == 0)
        def _():
            o_ref[...] = jnp.where(mine, packed, o_ref[...])


def _experts(tables, xs, w_gate, w_up, w_down):
    n_rows, half = xs.shape
    d = 2 * half
    tm = EXPERT_TILE
    rows = lambda p, tile, exp, *_: (tile[p], 0)
    wsel = lambda p, tile, exp, *_: (exp[p], 0, 0)
    return pl.pallas_call(
        _expert_kernel,
        grid_spec=pltpu.PrefetchScalarGridSpec(
            num_scalar_prefetch=len(tables),
            grid=(tables[0].shape[0],),
            in_specs=[
                pl.BlockSpec((tm, half), rows),
                pl.BlockSpec((1, d, EXPERT_FF), wsel),
                pl.BlockSpec((1, d, EXPERT_FF), wsel),
                pl.BlockSpec((1, EXPERT_FF, d), wsel),
            ],
            out_specs=pl.BlockSpec((tm, half), rows),
        ),
        out_shape=jax.ShapeDtypeStruct((n_rows, half), jnp.uint32),
        compiler_params=_params("arbitrary"),
        name="moe_experts",
    )(*tables, xs, w_gate, w_up, w_down)


def _combine_kernel(pos_ref, h_ref, gate_ref, g_ref, ys_ref, *refs, tb, n_steps, final):
    if final:
        o_ref, y_ref, sem = refs
    else:
        o_ref, on_ref, y_ref, sem = refs
    i = pl.program_id(0)
    s = tb * n_steps
    slot = lax.rem(i, 2)
    half = h_ref.shape[1] // 2

    def fetch(block, sl):
        base = block * tb

        def issue(t, carry):
            _row_copy(ys_ref, pos_ref[base + t], y_ref.at[sl, 0], t, sem.at[sl]).start()
            _row_copy(ys_ref, pos_ref[s + base + t], y_ref.at[sl, 1], t, sem.at[sl]).start()
            return carry

        lax.fori_loop(0, tb, issue, 0, unroll=DMA_ISSUE_UNROLL)

    @pl.when(i == 0)
    def _():
        fetch(0, 0)

    @pl.when(i + 1 < n_steps)
    def _():
        fetch(i + 1, 1 - slot)

    _wait_rows(ys_ref, y_ref.at[slot, 0], sem.at[slot], tb)
    _wait_rows(ys_ref, y_ref.at[slot, 1], sem.at[slot], tb)
    def combine_rows(r, carry):
        rows = pl.ds(pl.multiple_of(r * ROW_CHUNK, ROW_CHUNK), ROW_CHUNK)
        gates = gate_ref[rows, :]
        g1, g2 = gates[:, 0:1], gates[:, 1:2]
        hi1, lo1 = _unpack_bf16_pair(y_ref[slot, 0, rows, :])
        hi2, lo2 = _unpack_bf16_pair(y_ref[slot, 1, rows, :])
        new_hi = h_ref[rows, :half] + g1 * hi1 + g2 * hi2
        new_lo = h_ref[rows, half:] + g1 * lo1 + g2 * lo2
        sq = (jnp.sum(new_hi * new_hi, axis=-1, keepdims=True)
              + jnp.sum(new_lo * new_lo, axis=-1, keepdims=True))
        inv = lax.rsqrt(sq / (2 * half) + NORM_EPS)
        if final:
            o_ref[rows, :half] = new_hi * inv * g_ref[:, :half]
            o_ref[rows, half:] = new_lo * inv * g_ref[:, half:]
        else:
            o_ref[rows, :half] = new_hi
            o_ref[rows, half:] = new_lo
            on_ref[rows, :half] = (new_hi * inv * g_ref[:, :half]).astype(on_ref.dtype)
            on_ref[rows, half:] = (new_lo * inv * g_ref[:, half:]).astype(on_ref.dtype)
        return carry

    lax.fori_loop(0, tb // ROW_CHUNK, combine_rows, 0, unroll=4)


def _combine(pos, h, gates, g_norm, ys, tb, final):
    s, d = h.shape
    n_steps = s // tb
    row = lambda i, pos: (i, 0)
    out_specs = [pl.BlockSpec((tb, d), row)]
    out_shape = [jax.ShapeDtypeStruct((s, d), F32)]
    if not final:
        out_specs.append(pl.BlockSpec((tb, d), row))
        out_shape.append(jax.ShapeDtypeStruct((s, d), BF16))
    return pl.pallas_call(
        functools.partial(_combine_kernel, tb=tb, n_steps=n_steps, final=final),
        grid_spec=pltpu.PrefetchScalarGridSpec(
            num_scalar_prefetch=1,
            grid=(n_steps,),
            in_specs=[
                pl.BlockSpec((tb, d), row),
                pl.BlockSpec((tb, ROUTER_LANES), row),
                pl.BlockSpec((1, d), lambda i, pos: (0, 0)),
                pl.BlockSpec(memory_space=pl.ANY),
            ],
            out_specs=out_specs,
            scratch_shapes=[pltpu.VMEM((2, 2, tb, d // 2), jnp.uint32),
                            pltpu.SemaphoreType.DMA((2,))],
        ),
        out_shape=out_shape,
        compiler_params=_params("arbitrary"),
        name="moe_combine",
    )(pos, h, gates, g_norm, ys)


def _positions_kernel(idx_ref, start_ref, o_ref):
    idx = idx_ref[...]
    lane = lax.broadcasted_iota(jnp.int32, idx.shape, 1)
    start = start_ref[...]

    def column(c):
        return jnp.sum(jnp.where(lane == c, idx, 0), axis=-1, keepdims=True)

    def seg_start(e):
        return jnp.sum(jnp.where(lane == e, start, 0), axis=-1, keepdims=True)

    pos1 = seg_start(column(0)) + column(2)
    pos2 = seg_start(column(1)) + column(3)
    o_ref[...] = jnp.where(lane == 0, pos1, jnp.where(lane == 1, pos2, 0))


def _positions(idx, seg_start, tb):
    s = idx.shape[0]
    start_row = jnp.pad(seg_start, (0, ROUTER_LANES - N_EXPERTS)).reshape(1, ROUTER_LANES)
    out = pl.pallas_call(
        _positions_kernel,
        grid=(s // tb,),
        in_specs=[pl.BlockSpec((tb, ROUTER_LANES), lambda i: (i, 0)),
                  pl.BlockSpec((1, ROUTER_LANES), lambda i: (0, 0))],
        out_specs=pl.BlockSpec((tb, ROUTER_LANES), lambda i: (i, 0)),
        out_shape=jax.ShapeDtypeStruct((s, ROUTER_LANES), jnp.int32),
        compiler_params=_params("parallel"),
        name="moe_positions",
    )(idx, start_row)
    return jnp.concatenate([out[:, 0], out[:, 1]])


def _count_le(sorted_vals, queries):
    return jnp.sum(sorted_vals[None, :] <= queries[:, None], axis=1).astype(jnp.int32)


def _moe(h, g, g_next, final, idx, gates, counts, w_gate, w_up, w_down):
    s, d = h.shape

    tm = EXPERT_TILE
    n_tiles = 2 * s // tm
    n_pairs = n_tiles + N_EXPERTS - 1
    counts = counts[0, N_GROUPS:N_GROUPS + N_EXPERTS].astype(jnp.int32)
    seg_end = jnp.cumsum(counts)
    seg_start = seg_end - counts
    pos = _positions(idx, seg_start, tb=min(2048, s))
    tile_row = jnp.arange(n_tiles, dtype=jnp.int32) * tm
    first_e = _count_le(seg_end, tile_row)
    last_e = _count_le(seg_end, tile_row + tm - 1)
    per_tile = last_e - first_e + 1
    pair_end = jnp.cumsum(per_tile)
    pair_start = pair_end - per_tile
    total = pair_end[-1]
    p = jnp.minimum(jnp.arange(n_pairs, dtype=jnp.int32), total - 1)
    p_tile = _count_le(pair_end, p)
    p_expert = first_e[p_tile] + p - pair_start[p_tile]
    p_lo = jnp.clip(seg_start[p_expert] - p_tile * tm, 0, tm)
    p_hi = jnp.clip(seg_end[p_expert] - p_tile * tm, 0, tm)
    p_first = (p_expert == first_e[p_tile]).astype(jnp.int32)

    xs = _dispatch(pos, h, g, tb=min(256, s))
    ys = _experts((p_tile, p_expert, p_lo, p_hi, p_first, total.reshape(1)), xs,
                  w_gate, w_up, w_down)
    return _combine(pos, h, gates, g_next, ys, tb=min(256, s), final=final)


def _rope_tables(s):
    inv_freq = ROPE_THETA ** (-jnp.arange(0, MLA_ROPE_DIM, 2, dtype=F32) / MLA_ROPE_DIM)
    ang = jnp.arange(s, dtype=F32)[:, None] * inv_freq[None, :]
    cos, sin = jnp.cos(ang), jnp.sin(ang)
    zeros = jnp.zeros((s, 128 - MLA_ROPE_DIM), F32)
    return (jnp.concatenate([cos, cos, zeros], axis=1), jnp.concatenate([sin, sin, zeros], axis=1))


def _rot_half_cols(w):
    half = MLA_ROPE_DIM // 2
    return jnp.concatenate([-w[..., half:], w[..., :half]], axis=-1)


def _even_mixer(h, hn, w_in, pool_w, pool_scale, q_norm, w_uq, kv_norm, w_ukv, w_out, ck, sk,
                expert_weights):
    s, d = h.shape
    w_kpe = w_in[:, POOL_DIM + MLA_Q_RANK + MLA_KV_RANK:]
    w_in_ext = jnp.concatenate([w_in, _rot_half_cols(w_kpe)], axis=1).astype(BF16)
    big = min(1024, s)
    z = _matmul([hn], [w_in_ext], BF16, tm=big, tn=896, name="even_in_proj")

    ya = _pool_mixer(z, pool_w.astype(BF16), pool_scale.reshape(1, POOL_DIM), ts=min(512, s))

    w_q = w_uq.reshape(MLA_Q_RANK, MLA_HEADS, MLA_QK_DIM)
    w_q = jnp.concatenate([w_q, _rot_half_cols(w_q[..., MLA_NOPE_DIM:])], axis=-1)
    w_q = jnp.transpose(w_q, (1, 0, 2)).astype(BF16)
    w_kv = w_ukv.reshape(MLA_KV_RANK, MLA_HEADS // 2, 2, MLA_NOPE_DIM + MLA_V_DIM)
    w_k = jnp.transpose(w_kv[..., :MLA_NOPE_DIM], (1, 0, 2, 3))
    w_k = w_k.reshape(MLA_HEADS // 2, MLA_KV_RANK, 2 * MLA_NOPE_DIM).astype(BF16)
    w_vt = jnp.transpose(w_kv[..., MLA_NOPE_DIM:], (1, 2, 3, 0))
    w_vt = w_vt.reshape(MLA_HEADS // 2, 2 * MLA_V_DIM, MLA_KV_RANK).astype(BF16)
    q = _mla_q(z, q_norm.reshape(1, -1), w_q, ck, sk, tm=big, hb=4)
    k, vt = _mla_kv(z, kv_norm.reshape(1, -1), w_k, w_vt, ck, sk, tm=big, hb=4)
    yb, casts = _mla_attention(q, k, vt, tb=min(512, s), n_sub=4 if s >= 2048 else 1,
                               side=expert_weights)

    w_o = w_out.astype(BF16)
    h = _matmul([ya, yb], [w_o[:POOL_DIM], w_o[POOL_DIM:]], F32, tm=big, tn=512,
                residual=h, name="even_out_proj")
    return h, casts


def _odd_mixer(h, hn, w_in, ln_g, ln_b, w_s, b_s, sinks, rel_bias, w_out, expert_weights):
    s, d = h.shape
    big = min(1024, s)
    z, cast_b = _matmul([hn], [w_in.astype(BF16)], BF16, tm=big, tn=512, name="odd_in_proj",
                        side=expert_weights[1:])
    yc = _sgu_mixer(z, ln_g.reshape(1, -1), ln_b.reshape(1, -1), w_s, jnp.transpose(b_s),
                    ts=min(256, s))
    yd, cast_a = _swa_mixer(z, sinks, _swa_bias(rel_bias), side=expert_weights[:1])
    w_o = w_out.astype(BF16)
    h = _matmul([yc, yd], [w_o[:SGU_DIM], w_o[SGU_DIM:]], F32, tm=big, tn=512,
                residual=h, name="odd_out_proj")
    return h, cast_a + cast_b


def kernel(x, mem, norm_mix, norm_xattn, norm_ffn, norm_mem, final_norm, rel_bias, e_w_in, pool_w, pool_scale, mla_q_norm, mla_w_uq, mla_kv_norm, mla_w_ukv, e_w_out, o_w_in, sgu_ln_g, sgu_ln_b, sgu_w, sgu_b, swa_sinks, o_w_out, xa_wq, xa_wk, xa_wv, xa_wo, moe_w_group, moe_b_group, moe_w_expert, moe_b_expert, moe_w_gate, moe_w_up, moe_w_down):
    batch, s, d = x.shape
    assert batch == 1
    depth = norm_mix.shape[0]
    ck, sk = _rope_tables(s)
    h = x.reshape(s, d)
    mem2 = mem.reshape(mem.shape[1], d)
    hn = _rmsnorm(h, norm_mix[0], BF16, tm=min(512, s))
    for layer in range(depth):
        i = layer // 2
        last = layer == depth - 1
        expert_weights = [(moe_w_gate, layer), (moe_w_up, layer), (moe_w_down, layer)]
        if layer % 2 == 0:
            h, casts = _even_mixer(h, hn, e_w_in[i], pool_w[i], pool_scale[i], mla_q_norm[i],
                                   mla_w_uq[i], mla_kv_norm[i], mla_w_ukv[i], e_w_out[i], ck, sk,
                                   expert_weights)
        else:
            h, casts = _odd_mixer(h, hn, o_w_in[i], sgu_ln_g[i], sgu_ln_b[i], sgu_w[i], sgu_b[i],
                                  swa_sinks[i], rel_bias, o_w_out[i], expert_weights)
        mem_n = _rmsnorm(mem2, norm_mem[layer], BF16, tm=mem2.shape[0])
        w_kv_mem = jnp.concatenate([xa_wk[layer], xa_wv[layer]], axis=1).astype(BF16)
        kv_mem = _matmul([mem_n], [w_kv_mem], BF16, tm=mem2.shape[0], tn=2 * XATTN_DIM,
                         name="mem_kv_proj")
        g_ffn = norm_ffn[layer].reshape(1, d)
        router_weights = _router_weights(moe_w_group[layer], moe_b_group[layer],
                                         moe_w_expert[layer], moe_b_expert[layer])
        (h, idx, gates, counts), more_casts = _xattn_and_route(
            h, norm_xattn[layer].reshape(1, d), xa_wq[layer].astype(BF16), kv_mem,
            xa_wo[layer].astype(BF16), g_ffn, router_weights, tm=min(256, s),
            side=expert_weights[len(casts):])
        w_gate, w_up, w_down = casts + more_casts
        g_next = final_norm if last else norm_mix[layer + 1]
        res = _moe(h, g_ffn, g_next.reshape(1, d), last, idx, gates, counts, w_gate, w_up, w_down)
        if last:
            out, = res
        else:
            h, hn = res
    return out.reshape(batch, s, d)
```
